```python
import jax, jax.numpy as jnp
from jax import lax
import numpy as np

D_MODEL = 4096
BATCH = 4
SEQ = 4096
DEPTH = 1
DEC_BATCH = 16
DEC_SEQ = 64
PAST_LEN = 1024

CHUNK = 64
EPS = 1e-6
D_CONV = D_MODEL // 2
CONV_W = 3
HEAD_DIM = 128
N_HEADS = (D_MODEL // 2) // HEAD_DIM
N_KV_HEADS = 4
N_IDX_HEADS = 16
IDX_DIM = 64
TOPK_MAX = 256
Q_BLOCK = 128
N_GROUPS = 4
EXPERTS_PER_GROUP = 8
N_EXPERTS = N_GROUPS * EXPERTS_PER_GROUP
TOPK_EXPERTS = 2
D_EXPERT = 1024
MOE_BLOCK = 128

OFF_CONV = 0
OFF_Q = OFF_CONV + 3 * D_CONV
OFF_K = OFF_Q + N_HEADS * HEAD_DIM
OFF_V = OFF_K + N_KV_HEADS * HEAD_DIM
OFF_IQ = OFF_V + N_KV_HEADS * HEAD_DIM
OFF_IK = OFF_IQ + N_IDX_HEADS * IDX_DIM
OFF_IW = OFF_IK + IDX_DIM
OFF_GATE = OFF_IW + N_IDX_HEADS
D_IN = OFF_GATE + 2 * D_MODEL

kernel_name = "hybrid_conv_dsa_hmoe_stream_step"


def _rms(x):
    xf = x.astype(jnp.float32)
    return (xf * lax.rsqrt(jnp.mean(xf * xf, axis=-1, keepdims=True) + EPS)).astype(x.dtype)


def _modulate(x, g, shift, scale):
    return _rms(x) * g * (1 + scale[:, None]) + shift[:, None]


def _short_conv(h, bg, cg, conv_prev, conv_w):
    u = cg * h
    S = u.shape[1]
    up = jnp.concatenate([conv_prev.astype(u.dtype), u], axis=1)
    y = conv_w[0] * up[:, 0:S]
    for j in range(1, CONV_W):
        y = y + conv_w[j] * up[:, j:j + S]
    return bg * y, up[:, -(CONV_W - 1):]


def _dsa_attend(q, iq, iw, k, v, ik, visible, topk):
    dots = jnp.einsum('bqhd,bld->bqhl', iq, ik).astype(jnp.float32) * IDX_DIM ** -0.5
    score = jnp.einsum('bqh,bqhl->bql', iw.astype(jnp.float32), jax.nn.relu(dots))
    score = jnp.where(visible, score, -jnp.inf)
    vals, sel = lax.top_k(score, topk)
    keep = jnp.isfinite(vals)
    k_sel = jax.vmap(lambda kb, sb: kb[sb])(k, sel)
    v_sel = jax.vmap(lambda vb, sb: vb[sb])(v, sel)
    B, Q, H, hd = q.shape
    qg = q.reshape(B, Q, N_KV_HEADS, H // N_KV_HEADS, hd)
    logits = jnp.einsum('bqgrd,bqkgd->bqgrk', qg, k_sel).astype(jnp.float32) * hd ** -0.5
    logits = jnp.where(keep[:, :, None, None, :], logits, -jnp.inf)
    p = jax.nn.softmax(logits, axis=-1).astype(v.dtype)
    o = jnp.einsum('bqgrk,bqkgd->bqgrd', p, v_sel)
    return o.reshape(B, Q, H * hd)


def _prompt_attention(q, iq, iw, k, v, ik):
    B, S = q.shape[:2]
    topk = min(TOPK_MAX, S // 4)
    key_chunk = jnp.arange(S) // CHUNK

    def block(i):
        q0 = i * Q_BLOCK
        qb = lax.dynamic_slice_in_dim(q, q0, Q_BLOCK, axis=1)
        iqb = lax.dynamic_slice_in_dim(iq, q0, Q_BLOCK, axis=1)
        iwb = lax.dynamic_slice_in_dim(iw, q0, Q_BLOCK, axis=1)
        q_chunk = (q0 + jnp.arange(Q_BLOCK)) // CHUNK
        visible = (key_chunk[None, :] <= q_chunk[:, None])[None]
        return _dsa_attend(qb, iqb, iwb, k, v, ik, visible, topk)

    out = lax.map(block, jnp.arange(S // Q_BLOCK))
    return out.transpose(1, 0, 2, 3).reshape(B, S, -1)


def _cached_attention(q, iq, iw, k_all, v_all, ik_all):
    B, Sd = q.shape[:2]
    L = k_all.shape[1]
    topk = min(TOPK_MAX, L // 4)
    visible = jnp.ones((1, Sd, L), dtype=bool)
    return _dsa_attend(q, iq, iw, k_all, v_all, ik_all, visible, topk)


def _hier_moe(x, w_router_group, b_router_group, w_router_expert, b_router_expert, w_gate, w_up, w_down):
    B, S, D = x.shape
    xt = x.reshape(-1, D)
    T = xt.shape[0]
    g_logits = (xt @ w_router_group).astype(jnp.float32) + b_router_group.astype(jnp.float32)
    g_prob = jax.nn.softmax(g_logits, axis=-1)
    g_sel = jnp.argmax(g_logits, axis=-1)
    g_w = jnp.take_along_axis(g_prob, g_sel[:, None], axis=-1)[:, 0]
    e_logits = ((xt @ w_router_expert).astype(jnp.float32) + b_router_expert.astype(jnp.float32))
    e_logits = e_logits.reshape(T, N_GROUPS, EXPERTS_PER_GROUP)
    e_logits = jnp.take_along_axis(e_logits, g_sel[:, None, None], axis=1)[:, 0]
    e_prob = jax.nn.softmax(e_logits, axis=-1)
    e_w, e_loc = lax.top_k(e_prob, TOPK_EXPERTS)
    e_w = e_w / jnp.sum(e_w, axis=-1, keepdims=True)
    expert = g_sel[:, None] * EXPERTS_PER_GROUP + e_loc
    weight = g_w[:, None] * e_w
    n_slots = T * TOPK_EXPERTS
    slot_e = expert.reshape(-1)
    slot_tok = jnp.repeat(jnp.arange(T, dtype=jnp.int32), TOPK_EXPERTS)
    slot_w = weight.reshape(-1)
    order = jnp.argsort(slot_e)
    se, st, sw = slot_e[order], slot_tok[order], slot_w[order]
    counts = jnp.zeros((N_EXPERTS,), jnp.int32).at[slot_e].add(1)
    start = jnp.cumsum(counts) - counts
    padded = (counts + MOE_BLOCK - 1) // MOE_BLOCK * MOE_BLOCK
    pend = jnp.cumsum(padded)
    pstart = pend - padded
    dest = pstart[se] + (jnp.arange(n_slots, dtype=jnp.int32) - start[se])
    n_blocks = -(-n_slots // MOE_BLOCK) + N_EXPERTS
    buf = n_blocks * MOE_BLOCK
    buf_tok = jnp.zeros((buf,), jnp.int32).at[dest].set(st)
    buf_w = jnp.zeros((buf,), jnp.float32).at[dest].set(sw)
    block_e = jnp.minimum(jnp.searchsorted(pend, jnp.arange(n_blocks) * MOE_BLOCK, side='right'), N_EXPERTS - 1)

    def run(args):
        tok, e = args
        xb = xt[tok]
        hb = jax.nn.silu(xb @ w_gate[e]) * (xb @ w_up[e])
        return hb @ w_down[e]

    yb = lax.map(run, (buf_tok.reshape(n_blocks, MOE_BLOCK), block_e))
    yb = yb.reshape(buf, D) * buf_w[:, None].astype(x.dtype)
    y = jnp.zeros_like(xt).at[buf_tok].add(yb)
    return y.reshape(B, S, D)


def _layer(x, c, conv_prev, k_past, v_past, ik_past, norm1_g, norm2_g, w_ada, b_ada, w_in, conv_w,
           q_norm_g, k_norm_g, ik_norm_g, w_conv_out, w_attn_out, w_out, w_router_group, b_router_group,
           w_router_expert, b_router_expert, w_gate, w_up, w_down):
    B, S, _ = x.shape
    mod = c @ w_ada + b_ada
    sh1, sc1, g1, sh2, sc2, g2 = jnp.split(mod, 6, axis=-1)
    xm = _modulate(x, norm1_g, sh1, sc1)
    proj = xm @ w_in
    h_c = proj[..., OFF_CONV:OFF_CONV + D_CONV]
    b_c = proj[..., OFF_CONV + D_CONV:OFF_CONV + 2 * D_CONV]
    c_c = proj[..., OFF_CONV + 2 * D_CONV:OFF_Q]
    q = _rms(proj[..., OFF_Q:OFF_K].reshape(B, S, N_HEADS, HEAD_DIM)) * q_norm_g
    k = _rms(proj[..., OFF_K:OFF_V].reshape(B, S, N_KV_HEADS, HEAD_DIM)) * k_norm_g
    v = proj[..., OFF_V:OFF_IQ].reshape(B, S, N_KV_HEADS, HEAD_DIM)
    iq = proj[..., OFF_IQ:OFF_IK].reshape(B, S, N_IDX_HEADS, IDX_DIM)
    ik = _rms(proj[..., OFF_IK:OFF_IW]) * ik_norm_g
    iw = proj[..., OFF_IW:OFF_GATE] * N_IDX_HEADS ** -0.5
    gate = jax.nn.sigmoid(proj[..., OFF_GATE:])
    g_conv, g_attn = gate[..., :D_MODEL], gate[..., D_MODEL:]
    conv_y, conv_state = _short_conv(h_c, b_c, c_c, conv_prev, conv_w)
    if k_past is None:
        attn = _prompt_attention(q, iq, iw, k, v, ik)
    else:
        attn = _cached_attention(q, iq, iw,
                                 jnp.concatenate([k_past.astype(k.dtype), k], axis=1),
                                 jnp.concatenate([v_past.astype(v.dtype), v], axis=1),
                                 jnp.concatenate([ik_past.astype(ik.dtype), ik], axis=1))
    merged = g_conv * (conv_y @ w_conv_out) + g_attn * (attn @ w_attn_out)
    x = x + g1[:, None] * (merged @ w_out)
    xm2 = _modulate(x, norm2_g, sh2, sc2)
    y = x + g2[:, None] * _hier_moe(xm2, w_router_group, b_router_group, w_router_expert, b_router_expert,
                                    w_gate, w_up, w_down)
    return y, k, v, ik, conv_state


def setup_inputs(seed: int = 0) -> dict:
    key = jax.random.key(seed)
    ks = jax.random.split(key, 32)
    n = lambda i, shape, s=1.0: jax.random.normal(ks[i], shape, jnp.float32) * s
    return {
        "x_prompt": n(0, (BATCH, SEQ, D_MODEL)),
        "x_sample": n(1, (DEC_BATCH, DEC_SEQ, D_MODEL)),
        "cache_k": n(2, (DEC_BATCH, PAST_LEN, N_KV_HEADS, HEAD_DIM)),
        "cache_v": n(3, (DEC_BATCH, PAST_LEN, N_KV_HEADS, HEAD_DIM)),
        "cache_idx_k": n(4, (DEC_BATCH, PAST_LEN, IDX_DIM)),
        "state_conv": n(5, (DEC_BATCH, CONV_W - 1, D_CONV)),
        "c_prompt": n(6, (BATCH, D_MODEL)),
        "c_sample": n(7, (DEC_BATCH, D_MODEL)),
        "norm1_g": 1.0 + n(8, (D_MODEL,), 0.01),
        "norm2_g": 1.0 + n(9, (D_MODEL,), 0.01),
        "w_ada": n(10, (D_MODEL, 6 * D_MODEL), 0.5 * D_MODEL ** -0.5),
        "b_ada": n(11, (6 * D_MODEL,), 0.01),
        "w_in": n(12, (D_MODEL, D_IN), D_MODEL ** -0.5),
        "conv_w": n(13, (CONV_W, D_CONV), CONV_W ** -0.5),
        "q_norm_g": 1.0 + n(14, (HEAD_DIM,), 0.01),
        "k_norm_g": 1.0 + n(15, (HEAD_DIM,), 0.01),
        "ik_norm_g": 1.0 + n(16, (IDX_DIM,), 0.01),
        "w_conv_out": n(17, (D_CONV, D_MODEL), D_CONV ** -0.5),
        "w_attn_out": n(18, (N_HEADS * HEAD_DIM, D_MODEL), (N_HEADS * HEAD_DIM) ** -0.5),
        "w_out": n(19, (D_MODEL, D_MODEL), D_MODEL ** -0.5),
        "w_router_group": n(20, (D_MODEL, N_GROUPS), D_MODEL ** -0.5),
        "b_router_group": n(21, (N_GROUPS,), 0.01),
        "w_router_expert": n(22, (D_MODEL, N_EXPERTS), D_MODEL ** -0.5),
        "b_router_expert": n(23, (N_EXPERTS,), 0.01),
        "w_gate": n(24, (N_EXPERTS, D_MODEL, D_EXPERT), D_MODEL ** -0.5),
        "w_up": n(25, (N_EXPERTS, D_MODEL, D_EXPERT), D_MODEL ** -0.5),
        "w_down": n(26, (N_EXPERTS, D_EXPERT, D_MODEL), D_EXPERT ** -0.5),
    }


def reference(x_prompt, x_sample, cache_k, cache_v, cache_idx_k, state_conv, c_prompt, c_sample,
              norm1_g, norm2_g, w_ada, b_ada, w_in, conv_w, q_norm_g, k_norm_g, ik_norm_g,
              w_conv_out, w_attn_out, w_out, w_router_group, b_router_group, w_router_expert,
              b_router_expert, w_gate, w_up, w_down):
    weights = (norm1_g, norm2_g, w_ada, b_ada, w_in, conv_w, q_norm_g, k_norm_g, ik_norm_g,
               w_conv_out, w_attn_out, w_out, w_router_group, b_router_group, w_router_expert,
               b_router_expert, w_gate, w_up, w_down)
    conv_zero = jnp.zeros((x_prompt.shape[0], CONV_W - 1, D_CONV), x_prompt.dtype)
    y_prompt, k_prompt, v_prompt, idx_k_prompt, conv_prompt = _layer(
        x_prompt, c_prompt, conv_zero, None, None, None, *weights)
    y_sample, k_sample, v_sample, idx_k_sample, conv_sample = _layer(
        x_sample, c_sample, state_conv, cache_k, cache_v, cache_idx_k, *weights)
    return (y_prompt, y_sample, k_prompt, v_prompt, idx_k_prompt, conv_prompt,
            k_sample, v_sample, idx_k_sample, conv_sample)
```

```python
import functools

import jax
import jax.numpy as jnp
from jax import lax
from jax.experimental import pallas as pl
from jax.experimental.pallas import tpu as pltpu

F32 = jnp.float32
BF16 = jnp.bfloat16
I32 = jnp.int32

CHUNK = 64
EPS = 1e-6
TOPK_MAX = 256
TOPK_EXPERTS = 2
LANES = 128
VMEM_LIMIT = 56 * 1024 * 1024
INT_MIN = -2 ** 31
NEG = -1e30
MOE_BM = 256
NT_DIMS = (((1,), (1,)), ((), ()))


def _tile(n, prefs):
    for t in prefs:
        if n % t == 0:
            return t
    return n


def _cparams(*sem):
    return pltpu.CompilerParams(dimension_semantics=sem, vmem_limit_bytes=VMEM_LIMIT)


def _mod_operand(m, S, tm):
    B, D = m.shape
    if S % tm == 0:
        return m.reshape(B, 1, D), 1, (lambda i: (i * tm) // S)
    assert tm % S == 0
    full = jnp.broadcast_to(m[:, None, :], (B, S, D)).reshape((B * S) // tm, tm, D)
    return full, tm, (lambda i: i)


def _ada_kernel(c_ref, w_ref, b_ref, o_ref):
    o_ref[...] = jnp.dot(c_ref[...], w_ref[...], preferred_element_type=F32,
                         precision=lax.Precision.HIGHEST) + b_ref[...]


def _ada(c, w, b):
    M, D = c.shape
    N = w.shape[1]
    tn = _tile(N, (512, 256, 128))
    return pl.pallas_call(
        _ada_kernel, name="ada",
        out_shape=jax.ShapeDtypeStruct((M, N), F32),
        grid=(N // tn,),
        in_specs=[pl.BlockSpec((M, D), lambda j: (0, 0)),
                  pl.BlockSpec((D, tn), lambda j: (0, j)),
                  pl.BlockSpec((1, tn), lambda j: (0, j))],
        out_specs=pl.BlockSpec((M, tn), lambda j: (0, j)),
        compiler_params=_cparams("arbitrary"),
    )(c, w, b.reshape(1, N))


def _modulated(x, g, scale, shift):
    ms = jnp.mean(x * x, axis=-1, keepdims=True)
    return (x * lax.rsqrt(ms + EPS)) * g * (1.0 + scale) + shift


def _modulate_kernel(x_ref, g_ref, sh_ref, sc_ref, o_ref):
    o_ref[...] = _modulated(x_ref[...], g_ref[...], sc_ref[0], sh_ref[0]).astype(o_ref.dtype)


def _modulate(x2d, S, g, shift, scale):
    T, D = x2d.shape
    tm = _tile(S, (512, 256, 128, 64))
    sh, rows, bidx = _mod_operand(shift, S, tm)
    sc, _, _ = _mod_operand(scale, S, tm)
    mspec = pl.BlockSpec((1, rows, D), lambda i: (bidx(i), 0, 0))
    return pl.pallas_call(
        _modulate_kernel, name="modulate",
        out_shape=jax.ShapeDtypeStruct((T, D), BF16),
        grid=(T // tm,),
        in_specs=[pl.BlockSpec((tm, D), lambda i: (i, 0)),
                  pl.BlockSpec((1, D), lambda i: (0, 0)), mspec, mspec],
        out_specs=pl.BlockSpec((tm, D), lambda i: (i, 0)),
        compiler_params=_cparams("arbitrary"),
    )(x2d, g.reshape(1, D), sh, sc)


def _mm_kernel(a_ref, b_ref, o_ref, *, act):
    acc = jnp.dot(a_ref[...], b_ref[...], preferred_element_type=F32)
    if act == "sigmoid":
        acc = jax.nn.sigmoid(acc)
    o_ref[...] = acc.astype(o_ref.dtype)


def _matmul(a, b, out_dtype, act="none", name="matmul"):
    M, K = a.shape
    N = b.shape[1]
    tm = _tile(M, (1024, 512, 256, 128))
    tn = _tile(N, (512, 384, 256, 128))
    return pl.pallas_call(
        functools.partial(_mm_kernel, act=act), name=name,
        out_shape=jax.ShapeDtypeStruct((M, N), out_dtype),
        grid=(M // tm, N // tn),
        in_specs=[pl.BlockSpec((tm, K), lambda i, j: (i, 0)),
                  pl.BlockSpec((K, tn), lambda i, j: (0, j))],
        out_specs=pl.BlockSpec((tm, tn), lambda i, j: (i, j)),
        compiler_params=_cparams("arbitrary", "arbitrary"),
    )(a, b)


def _attn_post_kernel(raw_ref, qg_ref, kg_ref, ikg_ref,
                      q_ref, k_ref, v_ref, ik_ref, kb_ref, vb_ref, ikb_ref, iq_ref, iw_ref,
                      *, NH, KVH, hd, NIH, DI):
    def headnorm(col, g):
        seg = raw_ref[:, col:col + hd]
        ms = jnp.mean(seg * seg, axis=-1, keepdims=True)
        return seg * lax.rsqrt(ms + EPS) * g

    qg = qg_ref[...] * (hd ** -0.5)
    for h in range(NH):
        q_ref[:, h * hd:(h + 1) * hd] = (headnorm(h * hd, 1.0) * qg).astype(q_ref.dtype)
    off_k = NH * hd
    off_v = off_k + KVH * hd
    for h in range(KVH):
        kn = headnorm(off_k + h * hd, kg_ref[...])
        k_ref[:, h * hd:(h + 1) * hd] = kn
        kb_ref[:, h * hd:(h + 1) * hd] = kn.astype(kb_ref.dtype)
    v = raw_ref[:, off_v:off_v + KVH * hd]
    v_ref[...] = v
    vb_ref[...] = v.astype(vb_ref.dtype)
    off_iq = off_v + KVH * hd
    off_ik = off_iq + NIH * DI
    off_iw = off_ik + DI
    iq_ref[...] = (raw_ref[:, off_iq:off_ik] * (DI ** -0.5)).astype(iq_ref.dtype)
    ikr = raw_ref[:, off_ik:off_iw]
    ikn = ikr * lax.rsqrt(jnp.mean(ikr * ikr, axis=-1, keepdims=True) + EPS) * ikg_ref[...]
    ik_ref[...] = ikn
    ikb_ref[...] = ikn.astype(ikb_ref.dtype)
    iw_ref[...] = raw_ref[:, off_iw:off_iw + NIH] * (NIH ** -0.5)


def _attn_post(raw, q_norm_g, k_norm_g, ik_norm_g, NH, KVH, hd, NIH, DI):
    T, NA = raw.shape
    tr = _tile(T, (256, 128, 64))
    outs = [(NH * hd, BF16), (KVH * hd, F32), (KVH * hd, F32), (DI, F32),
            (KVH * hd, BF16), (KVH * hd, BF16), (DI, BF16), (NIH * DI, BF16), (NIH, F32)]
    return pl.pallas_call(
        functools.partial(_attn_post_kernel, NH=NH, KVH=KVH, hd=hd, NIH=NIH, DI=DI), name="attn_post",
        out_shape=[jax.ShapeDtypeStruct((T, n), dt) for n, dt in outs],
        grid=(T // tr,),
        in_specs=[pl.BlockSpec((tr, NA), lambda i: (i, 0)),
                  pl.BlockSpec((1, hd), lambda i: (0, 0)),
                  pl.BlockSpec((1, hd), lambda i: (0, 0)),
                  pl.BlockSpec((1, DI), lambda i: (0, 0))],
        out_specs=[pl.BlockSpec((tr, n), lambda i: (i, 0)) for n, _ in outs],
        compiler_params=_cparams("arbitrary"),
    )(raw, q_norm_g.reshape(1, hd), k_norm_g.reshape(1, hd), ik_norm_g.reshape(1, DI))


def _conv_kernel(h_ref, b_ref, c_ref, w_ref, prev_ref, y_ref, st_ref, ubuf, *, ts, CW):
    lo = 8 - (CW - 1)

    @pl.when(pl.program_id(1) == 0)
    def _():
        ubuf[lo:8, :] = prev_ref[0]

    @pl.when(pl.program_id(1) > 0)
    def _():
        ubuf[lo:8, :] = ubuf[ts + lo:ts + 8, :]

    ubuf[8:8 + ts, :] = c_ref[0].astype(F32) * h_ref[0].astype(F32)
    y = w_ref[0:1, :] * ubuf[lo:lo + ts, :]
    for j in range(1, CW):
        y = y + w_ref[j:j + 1, :] * ubuf[lo + j:lo + j + ts, :]
    y_ref[0] = (b_ref[0].astype(F32) * y).astype(y_ref.dtype)
    st_ref[0] = ubuf[ts + lo:ts + 8, :]


def _short_conv(hbc, conv_prev, conv_w):
    B, S, C3 = hbc.shape
    CW, C = conv_w.shape
    ts = _tile(S, (512, 256, 128, 64))
    col = lambda k: pl.BlockSpec((1, ts, C), lambda b, s: (b, s, k))
    return pl.pallas_call(
        functools.partial(_conv_kernel, ts=ts, CW=CW), name="short_conv",
        out_shape=[jax.ShapeDtypeStruct((B, S, C), BF16), jax.ShapeDtypeStruct((B, CW - 1, C), F32)],
        grid=(B, S // ts),
        in_specs=[col(0), col(1), col(2),
                  pl.BlockSpec((CW, C), lambda b, s: (0, 0)),
                  pl.BlockSpec((1, CW - 1, C), lambda b, s: (b, 0, 0))],
        out_specs=[pl.BlockSpec((1, ts, C), lambda b, s: (b, s, 0)),
                   pl.BlockSpec((1, CW - 1, C), lambda b, s: (b, 0, 0))],
        scratch_shapes=[pltpu.VMEM((ts + 8, C), F32)],
        compiler_params=_cparams("arbitrary", "arbitrary"),
    )(hbc, hbc, hbc, conv_w, conv_prev)


def _attn_kernel(q_ref, iq_ref, iw_ref, k_ref, v_ref, ik_ref, o_ref, keys_ref, m_sc, l_sc, acc_sc,
                 *, QB, TL, NH, KVH, hd, NIH, DI, topk, causal, L_true, Lp):
    i = pl.program_id(1)
    rep = NH // KVH
    if causal:
        row = i * QB + lax.broadcasted_iota(I32, (QB, 1), 0)
        vis = (row // CHUNK + 1) * CHUNK
        nkt = ((i + 1) * QB + TL - 1) // TL
    else:
        vis = jnp.full((QB, 1), L_true, I32)
        nkt = Lp // TL

    def score_tile(t, carry):
        off = pl.multiple_of(t * TL, TL)
        ikt = ik_ref[0, pl.ds(off, TL), :]
        acc = jnp.zeros((QB, TL), F32)
        for h in range(NIH):
            d = lax.dot_general(iq_ref[0, :, h * DI:(h + 1) * DI], ikt, NT_DIMS, preferred_element_type=F32)
            acc = acc + iw_ref[0, :, h:h + 1] * jnp.maximum(d, 0.0)
        bits = pltpu.bitcast(acc, I32)
        key = bits ^ ((bits >> 31) & jnp.int32(0x7FFFFFFF))
        colid = off + lax.broadcasted_iota(I32, (1, TL), 1)
        keys_ref[:, pl.ds(off, TL)] = jnp.where(colid < vis, key, jnp.int32(INT_MIN))
        return carry

    lax.fori_loop(0, nkt, score_tile, 0)

    def count_ge(cand):
        def body(t, c):
            off = pl.multiple_of(t * TL, TL)
            kt = keys_ref[:, pl.ds(off, TL)]
            for s in range(TL // LANES):
                c = c + jnp.where(kt[:, s * LANES:(s + 1) * LANES] >= cand, 1, 0).astype(I32)
            return c
        c = lax.fori_loop(0, nkt, body, jnp.zeros((QB, LANES), I32))
        return jnp.sum(c, axis=1, keepdims=True)

    def bit_step(bi, thr):
        cand = thr + (jnp.int32(1) << (31 - bi))
        return jnp.where(count_ge(cand) >= topk, cand, thr)

    thr = lax.fori_loop(0, 32, bit_step, jnp.full((QB, 1), INT_MIN, I32))
    thr = jnp.maximum(thr, jnp.int32(INT_MIN + 1))

    for g in range(KVH):
        qg = jnp.concatenate([q_ref[0, :, (g * rep + r) * hd:(g * rep + r + 1) * hd] for r in range(rep)], axis=0)
        m_sc[...] = jnp.full(m_sc.shape, NEG, F32)
        l_sc[...] = jnp.zeros(l_sc.shape, F32)
        acc_sc[...] = jnp.zeros(acc_sc.shape, F32)

        def kv_tile(t, carry):
            off = pl.multiple_of(t * TL, TL)
            kt = k_ref[0, pl.ds(off, TL), g * hd:(g + 1) * hd]
            vt = v_ref[0, pl.ds(off, TL), g * hd:(g + 1) * hd]
            s = lax.dot_general(qg, kt, NT_DIMS, preferred_element_type=F32)
            bias = jnp.where(keys_ref[:, pl.ds(off, TL)] >= thr, 0.0, NEG)
            s = (s.reshape(rep, QB, TL) + bias[None]).reshape(rep * QB, TL)
            m_old = m_sc[...]
            m_new = jnp.maximum(m_old, jnp.max(s, axis=-1, keepdims=True))
            alpha = jnp.exp(m_old - m_new)
            p = jnp.exp(s - m_new)
            l_sc[...] = alpha * l_sc[...] + jnp.sum(p, axis=-1, keepdims=True)
            acc_sc[...] = alpha * acc_sc[...] + jnp.dot(p.astype(vt.dtype), vt, preferred_element_type=F32)
            m_sc[...] = m_new
            return carry

        lax.fori_loop(0, nkt, kv_tile, 0)
        o = acc_sc[...] / l_sc[...]
        for r in range(rep):
            o_ref[0, :, (g * rep + r) * hd:(g * rep + r + 1) * hd] = o[r * QB:(r + 1) * QB].astype(o_ref.dtype)


def _attention(q, iq, iw, k_all, v_all, ik_all, *, NH, KVH, hd, NIH, DI, topk, causal, L_true):
    B, S, _ = q.shape
    Lp = k_all.shape[1]
    QB = _tile(S, (128, 64))
    TL = _tile(Lp, (512, 384, 256, 128))
    rep = NH // KVH
    qspec = lambda n: pl.BlockSpec((1, QB, n), lambda b, i: (b, i, 0))
    kspec = lambda n: pl.BlockSpec((1, Lp, n), lambda b, i: (b, 0, 0))
    return pl.pallas_call(
        functools.partial(_attn_kernel, QB=QB, TL=TL, NH=NH, KVH=KVH, hd=hd, NIH=NIH, DI=DI,
                          topk=topk, causal=causal, L_true=L_true, Lp=Lp), name="sparse_attn",
        out_shape=jax.ShapeDtypeStruct((B, S, NH * hd), BF16),
        grid=(B, S // QB),
        in_specs=[qspec(NH * hd), qspec(NIH * DI), qspec(NIH), kspec(KVH * hd), kspec(KVH * hd), kspec(DI)],
        out_specs=qspec(NH * hd),
        scratch_shapes=[pltpu.VMEM((QB, Lp), I32), pltpu.VMEM((rep * QB, 1), F32),
                        pltpu.VMEM((rep * QB, 1), F32), pltpu.VMEM((rep * QB, hd), F32)],
        compiler_params=_cparams("arbitrary", "arbitrary"),
    )(q, iq, iw, k_all, v_all, ik_all)


def _merge_kernel(cy_ref, at_ref, wc_ref, wa_ref, gc_ref, ga_ref, o_ref):
    a = jnp.dot(cy_ref[...], wc_ref[...], preferred_element_type=F32)
    b = jnp.dot(at_ref[...], wa_ref[...], preferred_element_type=F32)
    o_ref[...] = (gc_ref[...].astype(F32) * a + ga_ref[...].astype(F32) * b).astype(o_ref.dtype)


def _merge(conv_y, attn, w_conv_out, w_attn_out, gates):
    T, C = conv_y.shape
    A = attn.shape[1]
    D = w_conv_out.shape[1]
    tm = _tile(T, (1024, 512, 256, 128))
    tn = _tile(D, (512, 256, 128))
    nj = D // tn
    return pl.pallas_call(
        _merge_kernel, name="merge",
        out_shape=jax.ShapeDtypeStruct((T, D), BF16),
        grid=(T // tm, nj),
        in_specs=[pl.BlockSpec((tm, C), lambda i, j: (i, 0)),
                  pl.BlockSpec((tm, A), lambda i, j: (i, 0)),
                  pl.BlockSpec((C, tn), lambda i, j: (0, j)),
                  pl.BlockSpec((A, tn), lambda i, j: (0, j)),
                  pl.BlockSpec((tm, tn), lambda i, j: (i, j)),
                  pl.BlockSpec((tm, tn), lambda i, j: (i, j + nj))],
        out_specs=pl.BlockSpec((tm, tn), lambda i, j: (i, j)),
        compiler_params=_cparams("arbitrary", "arbitrary"),
    )(conv_y, attn, w_conv_out, w_attn_out, gates, gates)


def _outproj_kernel(a_ref, w_ref, x_ref, g_ref, o_ref):
    acc = jnp.dot(a_ref[...], w_ref[...], preferred_element_type=F32)
    o_ref[...] = x_ref[...] + g_ref[0] * acc


def _outproj(merged, w_out, x2d, S, g1):
    T, D = x2d.shape
    K = merged.shape[1]
    tm = _tile(T, (1024, 512, 256, 128))
    tn = _tile(D, (512, 256, 128))
    g, rows, bidx = _mod_operand(g1, S, tm)
    return pl.pallas_call(
        _outproj_kernel, name="outproj",
        out_shape=jax.ShapeDtypeStruct((T, D), F32),
        grid=(T // tm, D // tn),
        in_specs=[pl.BlockSpec((tm, K), lambda i, j: (i, 0)),
                  pl.BlockSpec((K, tn), lambda i, j: (0, j)),
                  pl.BlockSpec((tm, tn), lambda i, j: (i, j)),
                  pl.BlockSpec((1, rows, tn), lambda i, j: (bidx(i), 0, j))],
        out_specs=pl.BlockSpec((tm, tn), lambda i, j: (i, j)),
        compiler_params=_cparams("arbitrary", "arbitrary"),
    )(merged, w_out, x2d, g)


def _router_kernel(x_ref, g_ref, sh_ref, sc_ref, wr_ref, br_ref, xm_ref, eid_ref, ew_ref, *, NG, NE):
    xm = _modulated(x_ref[...], g_ref[...], sc_ref[0], sh_ref[0])
    xm_ref[...] = xm
    logits = jnp.dot(xm, wr_ref[...], preferred_element_type=F32,
                     precision=lax.Precision.HIGHEST) + br_ref[...]
    tm = logits.shape[0]
    epg = NE // NG
    lane = lax.broadcasted_iota(I32, (tm, LANES), 1)
    ninf = -jnp.inf

    def first_argmax(vals):
        mx = jnp.max(vals, axis=1, keepdims=True)
        return mx, jnp.min(jnp.where(vals == mx, lane, LANES), axis=1, keepdims=True)

    gmask = lane < NG
    gmax, g_sel = first_argmax(jnp.where(gmask, logits, ninf))
    g_w = 1.0 / jnp.sum(jnp.where(gmask, jnp.exp(logits - gmax), 0.0), axis=1, keepdims=True)
    emask = (lane >= NG) & (lane < NG + NE) & ((lane - NG) // epg == g_sel)
    el = jnp.where(emask, logits, ninf)
    m1, i1 = first_argmax(el)
    m2, i2 = first_argmax(jnp.where(lane == i1, ninf, el))
    r = jnp.exp(m2 - m1)
    w1 = g_w / (1.0 + r)
    w2 = w1 * r
    eid_ref[...] = jnp.where(lane == 0, i1 - NG, jnp.where(lane == 1, i2 - NG, 0))
    ew_ref[...] = jnp.where(lane == 0, w1, jnp.where(lane == 1, w2, 0.0))


def _router(x2d, S, g, shift, scale, wr, br, NG, NE):
    T, D = x2d.shape
    tm = _tile(S, (256, 128, 64))
    sh, rows, bidx = _mod_operand(shift, S, tm)
    sc, _, _ = _mod_operand(scale, S, tm)
    mspec = pl.BlockSpec((1, rows, D), lambda i: (bidx(i), 0, 0))
    rowspec = lambda n: pl.BlockSpec((tm, n), lambda i: (i, 0))
    return pl.pallas_call(
        functools.partial(_router_kernel, NG=NG, NE=NE), name="router",
        out_shape=[jax.ShapeDtypeStruct((T, D), F32), jax.ShapeDtypeStruct((T, LANES), I32),
                   jax.ShapeDtypeStruct((T, LANES), F32)],
        grid=(T // tm,),
        in_specs=[rowspec(D), pl.BlockSpec((1, D), lambda i: (0, 0)), mspec, mspec,
                  pl.BlockSpec((D, LANES), lambda i: (0, 0)), pl.BlockSpec((1, LANES), lambda i: (0, 0))],
        out_specs=[rowspec(D), rowspec(LANES), rowspec(LANES)],
        compiler_params=_cparams("arbitrary"),
    )(x2d, g.reshape(1, D), sh, sc, wr, br)


def _gather_kernel(tok_ref, xa_hbm, xb_hbm, o_hbm, zbuf, sem, *, R, Ta):
    base = pl.program_id(0) * R

    def row_copy(r, src, t):
        return pltpu.make_async_copy(src.at[pl.ds(t, 1)], o_hbm.at[pl.ds(base + r, 1)], sem)

    def start(r, carry):
        t = tok_ref[0, 0, r]

        @pl.when(t < Ta)
        def _():
            row_copy(r, xa_hbm, t).start()

        @pl.when(t >= Ta)
        def _():
            row_copy(r, xb_hbm, t - Ta).start()
        return carry

    def wait(r, carry):
        row_copy(r, xa_hbm, 0).wait()
        return carry

    @pl.when(tok_ref[0, 0, 0] >= 0)
    def _():
        lax.fori_loop(0, R, start, 0)
        lax.fori_loop(0, R, wait, 0)

    @pl.when(tok_ref[0, 0, 0] < 0)
    def _():
        zbuf[...] = jnp.zeros(zbuf.shape, zbuf.dtype)
        fill = pltpu.make_async_copy(zbuf, o_hbm.at[pl.ds(base, R)], sem)
        fill.start()
        fill.wait()


def _moe_gather(tok_buf, xa, xb):
    P = tok_buf.shape[0]
    D = xa.shape[1]
    R = _tile(P, (256, 128, 64))
    return pl.pallas_call(
        functools.partial(_gather_kernel, R=R, Ta=xa.shape[0]), name="moe_gather",
        out_shape=jax.ShapeDtypeStruct((P, D), xa.dtype),
        grid=(P // R,),
        in_specs=[pl.BlockSpec((1, 1, R), lambda i: (i, 0, 0), memory_space=pltpu.SMEM),
                  pl.BlockSpec(memory_space=pl.ANY), pl.BlockSpec(memory_space=pl.ANY)],
        out_specs=pl.BlockSpec(memory_space=pl.ANY),
        scratch_shapes=[pltpu.VMEM((R, D), xa.dtype), pltpu.SemaphoreType.DMA(())],
        compiler_params=_cparams("arbitrary"),
    )(tok_buf.reshape(P // R, 1, R), xa, xb)


def _expert_kernel(be_ref, nu_ref, xs_ref, wg_ref, wu_ref, wd_ref, o_ref):
    i = pl.program_id(0)
    j = pl.program_id(1)

    @pl.when(j == 0)
    def _():
        o_ref[...] = jnp.zeros(o_ref.shape, o_ref.dtype)

    @pl.when(i < nu_ref[0])
    def _():
        x = xs_ref[...].astype(BF16)
        gate = jnp.dot(x, wg_ref[0].astype(BF16), preferred_element_type=F32)
        up = jnp.dot(x, wu_ref[0].astype(BF16), preferred_element_type=F32)
        hidden = (gate * jax.nn.sigmoid(gate)) * up
        o_ref[...] += jnp.dot(hidden.astype(BF16), wd_ref[0].astype(BF16), preferred_element_type=F32)


def _moe_experts(block_e, nused, xs, w_gate, w_up, w_down, BM):
    P, D = xs.shape
    NE, _, DE = w_gate.shape
    tf = _tile(DE, (256, 128))
    nf = DE // tf
    row = lambda i, nu: jnp.minimum(i, nu[0] - 1)
    fj = lambda i, j, nu: jnp.where(i < nu[0], j, nf - 1)
    grid_spec = pltpu.PrefetchScalarGridSpec(
        num_scalar_prefetch=2,
        grid=(P // BM, nf),
        in_specs=[pl.BlockSpec((BM, D), lambda i, j, be, nu: (row(i, nu), 0)),
                  pl.BlockSpec((1, D, tf), lambda i, j, be, nu: (be[row(i, nu)], 0, fj(i, j, nu))),
                  pl.BlockSpec((1, D, tf), lambda i, j, be, nu: (be[row(i, nu)], 0, fj(i, j, nu))),
                  pl.BlockSpec((1, tf, D), lambda i, j, be, nu: (be[row(i, nu)], fj(i, j, nu), 0))],
        out_specs=pl.BlockSpec((BM, D), lambda i, j, be, nu: (i, 0)),
    )
    return pl.pallas_call(
        _expert_kernel, name="moe_experts", grid_spec=grid_spec,
        out_shape=jax.ShapeDtypeStruct((P, D), F32),
        compiler_params=_cparams("arbitrary", "arbitrary"),
    )(block_e, nused, xs, w_gate, w_up, w_down)


def _combine_kernel(pos_ref, x_ref, g_ref, ew_ref, ys_hbm, o_ref, buf, sem, *, tc):
    def row_copy(r, jj, p):
        return pltpu.make_async_copy(ys_hbm.at[pl.ds(p, 1)], buf.at[jj, pl.ds(r, 1)], sem)

    def start(r, carry):
        for jj in range(TOPK_EXPERTS):
            row_copy(r, jj, pos_ref[0, 0, TOPK_EXPERTS * r + jj]).start()
        return carry

    lax.fori_loop(0, tc, start, 0)

    def wait(r, carry):
        for jj in range(TOPK_EXPERTS):
            row_copy(r, jj, 0).wait()
        return carry

    lax.fori_loop(0, tc, wait, 0)
    moe = ew_ref[:, 0:1] * buf[0] + ew_ref[:, 1:2] * buf[1]
    o_ref[...] = x_ref[...] + g_ref[0] * moe


def _moe_combine(pos, x2d, S, g2, ew, ys):
    T, D = x2d.shape
    tc = _tile(T, (256, 128, 64))
    g, rows, bidx = _mod_operand(g2, S, tc)
    return pl.pallas_call(
        functools.partial(_combine_kernel, tc=tc), name="moe_combine",
        out_shape=jax.ShapeDtypeStruct((T, D), F32),
        grid=(T // tc,),
        in_specs=[pl.BlockSpec((1, 1, TOPK_EXPERTS * tc), lambda i: (i, 0, 0), memory_space=pltpu.SMEM),
                  pl.BlockSpec((tc, D), lambda i: (i, 0)),
                  pl.BlockSpec((1, rows, D), lambda i: (bidx(i), 0, 0)),
                  pl.BlockSpec((tc, LANES), lambda i: (i, 0)),
                  pl.BlockSpec(memory_space=pl.ANY)],
        out_specs=pl.BlockSpec((tc, D), lambda i: (i, 0)),
        scratch_shapes=[pltpu.VMEM((TOPK_EXPERTS, tc, D), F32), pltpu.SemaphoreType.DMA(())],
        compiler_params=_cparams("arbitrary"),
    )(pos.reshape(T // tc, 1, TOPK_EXPERTS * tc), x2d, g, ew, ys)


def _route_meta(eid, NE, BM):
    n_slots = eid.shape[0] * TOPK_EXPERTS
    slot_e = eid.reshape(-1)
    order = jnp.argsort(slot_e, stable=True).astype(I32)
    se = slot_e[order]
    experts = jnp.arange(NE, dtype=I32)
    start = jnp.searchsorted(se, experts, side="left").astype(I32)
    counts = jnp.searchsorted(se, experts, side="right").astype(I32) - start
    padded = (counts + BM - 1) // BM * BM
    pend = jnp.cumsum(padded)
    pstart = pend - padded
    P = -(-n_slots // BM) * BM + NE * BM
    nblk = P // BM
    block_e = jnp.minimum(jnp.searchsorted(pend, jnp.arange(nblk, dtype=I32) * BM, side="right"), NE - 1).astype(I32)
    nused = (pend[-1] // BM).astype(I32).reshape(1)
    prow = jnp.arange(P, dtype=I32)
    pe = block_e[prow // BM]
    rank = prow - pstart[pe]
    valid = rank < counts[pe]
    src = jnp.clip(start[pe] + rank, 0, n_slots - 1)
    tok_buf = jnp.where(valid, order[src] // TOPK_EXPERTS, 0)
    tok_buf = jnp.where(prow < pend[-1], tok_buf, -1).astype(I32)
    dest_sorted = pstart[se] + (jnp.arange(n_slots, dtype=I32) - start[se])
    pos = dest_sorted[jnp.argsort(order).astype(I32)].astype(I32)
    return tok_buf, pos, block_e, nused


def _layer_front(x, mod, conv_prev, past, W, dims):
    B, S, D = x.shape
    T = B * S
    NH, KVH, hd, NIH, DI, NG, NE = dims
    sh1, sc1, g1, sh2, sc2, g2 = jnp.split(mod, 6, axis=-1)
    x2d = x.reshape(T, D)
    xm = _modulate(x2d, S, W["norm1_g"], sh1, sc1)
    hbc = _matmul(xm, W["w_in_conv"], BF16, name="inproj_conv")
    raw = _matmul(xm, W["w_in_attn"], F32, name="inproj_attn")
    gates = _matmul(xm, W["w_in_gate"], BF16, act="sigmoid", name="inproj_gate")
    q, k, v, ik, kb, vb, ikb, iq, iw = _attn_post(raw, W["q_norm_g"], W["k_norm_g"], W["ik_norm_g"],
                                                  NH, KVH, hd, NIH, DI)
    C = W["conv_w"].shape[1]
    conv_y, conv_state = _short_conv(hbc.reshape(B, S, 3 * C), conv_prev, W["conv_w"])
    r3 = lambda a: a.reshape(B, S, a.shape[-1])
    if past is None:
        k_all, v_all, ik_all = r3(kb), r3(vb), r3(ikb)
        L_true, causal = S, True
    else:
        k_past, v_past, ik_past = past
        Pn = k_past.shape[1]
        L_true, causal = Pn + S, False
        Lp = -(-L_true // LANES) * LANES
        cat = lambda old, new: jnp.pad(
            jnp.concatenate([old.reshape(B, Pn, -1).astype(BF16), r3(new)], axis=1),
            ((0, 0), (0, Lp - L_true), (0, 0)))
        k_all, v_all, ik_all = cat(k_past, kb), cat(v_past, vb), cat(ik_past, ikb)
    attn = _attention(r3(q), r3(iq), r3(iw), k_all, v_all, ik_all, NH=NH, KVH=KVH, hd=hd, NIH=NIH, DI=DI,
                      topk=min(TOPK_MAX, L_true // 4), causal=causal, L_true=L_true)
    merged = _merge(conv_y.reshape(T, C), attn.reshape(T, NH * hd), W["w_conv_out"], W["w_attn_out"], gates)
    x_mid = _outproj(merged, W["w_out"], x2d, S, g1)
    xm2, eid, ew = _router(x_mid, S, W["norm2_g"], sh2, sc2, W["w_router"], W["b_router"], NG, NE)
    outs = (k.reshape(B, S, KVH, hd), v.reshape(B, S, KVH, hd), ik.reshape(B, S, DI), conv_state)
    return x_mid, xm2, eid[:, :TOPK_EXPERTS], ew, g2, outs


def kernel(x_prompt, x_sample, cache_k, cache_v, cache_idx_k, state_conv, c_prompt, c_sample, norm1_g, norm2_g, w_ada, b_ada, w_in, conv_w, q_norm_g, k_norm_g, ik_norm_g, w_conv_out, w_attn_out, w_out, w_router_group, b_router_group, w_router_expert, b_router_expert, w_gate, w_up, w_down):
    Bp, Sp, D = x_prompt.shape
    Bs, Ss, _ = x_sample.shape
    CW, C = conv_w.shape
    hd = q_norm_g.shape[0]
    DI = ik_norm_g.shape[0]
    KVH = cache_k.shape[2]
    NH = w_attn_out.shape[0] // hd
    NG = b_router_group.shape[0]
    NE = b_router_expert.shape[0]
    off_q = 3 * C
    off_iq = off_q + NH * hd + 2 * KVH * hd
    off_gate = w_in.shape[1] - 2 * D
    NIH = (off_gate - off_iq - DI) // (DI + 1)
    dims = (NH, KVH, hd, NIH, DI, NG, NE)

    na = off_gate - off_q
    na_pad = -(-na // LANES) * LANES
    rpad = LANES - NG - NE
    W = {
        "norm1_g": norm1_g, "norm2_g": norm2_g, "conv_w": conv_w,
        "q_norm_g": q_norm_g, "k_norm_g": k_norm_g, "ik_norm_g": ik_norm_g,
        "w_in_conv": w_in[:, :off_q].astype(BF16),
        "w_in_attn": jnp.pad(w_in[:, off_q:off_gate], ((0, 0), (0, na_pad - na))).astype(BF16),
        "w_in_gate": w_in[:, off_gate:].astype(BF16),
        "w_conv_out": w_conv_out.astype(BF16), "w_attn_out": w_attn_out.astype(BF16),
        "w_out": w_out.astype(BF16),
        "w_router": jnp.pad(jnp.concatenate([w_router_group, w_router_expert], axis=1), ((0, 0), (0, rpad))),
        "b_router": jnp.pad(jnp.concatenate([b_router_group, b_router_expert]), (0, rpad)).reshape(1, LANES),
    }

    nb = Bp + Bs
    c_all = jnp.pad(jnp.concatenate([c_prompt, c_sample], axis=0), ((0, -nb % 8), (0, 0)))
    mod = _ada(c_all, w_ada, b_ada)

    conv_zero = jnp.zeros((Bp, CW - 1, C), F32)
    xmid_p, xm2_p, eid_p, ew_p, g2_p, outs_p = _layer_front(x_prompt, mod[:Bp], conv_zero, None, W, dims)
    xmid_s, xm2_s, eid_s, ew_s, g2_s, outs_s = _layer_front(
        x_sample, mod[Bp:nb], state_conv, (cache_k, cache_v, cache_idx_k), W, dims)

    Tp = Bp * Sp
    tok_buf, pos, block_e, nused = _route_meta(jnp.concatenate([eid_p, eid_s], axis=0), NE, MOE_BM)
    xs = _moe_gather(tok_buf, xm2_p, xm2_s)
    ys = _moe_experts(block_e, nused, xs, w_gate, w_up, w_down, MOE_BM)
    y_p = _moe_combine(pos[:TOPK_EXPERTS * Tp], xmid_p, Sp, g2_p, ew_p, ys).reshape(Bp, Sp, D)
    y_s = _moe_combine(pos[TOPK_EXPERTS * Tp:], xmid_s, Ss, g2_s, ew_s, ys).reshape(Bs, Ss, D)

    k_p, v_p, ik_p, conv_p = outs_p
    k_s, v_s, ik_s, conv_s = outs_s
    return (y_p, y_s, k_p, v_p, ik_p, conv_p, k_s, v_s, ik_s, conv_s)
```

```python
import functools

import jax
import jax.numpy as jnp
from jax import lax
from jax.experimental import pallas as pl
from jax.experimental.pallas import tpu as pltpu

F32 = jnp.float32
BF16 = jnp.bfloat16
I32 = jnp.int32

CHUNK = 64
EPS = 1e-6
TOPK_MAX = 256
TOPK_EXPERTS = 2
LANES = 128
VMEM_LIMIT = 56 * 1024 * 1024
INT_MIN = -2 ** 31
NEG = -1e30
LOG2E = 1.4426950408889634
MOE_BM = 512
NT_DIMS = (((1,), (1,)), ((), ()))


def _tile(n, prefs):
    for t in prefs:
        if n % t == 0:
            return t
    return n


def _cparams(*sem):
    return pltpu.CompilerParams(dimension_semantics=sem, vmem_limit_bytes=VMEM_LIMIT)


def _mod_operand(m, S, tm):
    B, D = m.shape
    if S % tm == 0:
        return m.reshape(B, 1, D), 1, (lambda i: (i * tm) // S)
    assert tm % S == 0
    full = jnp.broadcast_to(m[:, None, :], (B, S, D)).reshape((B * S) // tm, tm, D)
    return full, tm, (lambda i: i)


def _ada_kernel(c_ref, w_ref, b_ref, o_ref):
    o_ref[...] = jnp.dot(c_ref[...], w_ref[...], preferred_element_type=F32,
                         precision=lax.Precision.HIGHEST) + b_ref[...]


def _ada(c, w, b):
    M, D = c.shape
    N = w.shape[1]
    tn = _tile(N, (512, 256, 128))
    return pl.pallas_call(
        _ada_kernel, name="ada",
        out_shape=jax.ShapeDtypeStruct((M, N), F32),
        grid=(N // tn,),
        in_specs=[pl.BlockSpec((M, D), lambda j: (0, 0)),
                  pl.BlockSpec((D, tn), lambda j: (0, j)),
                  pl.BlockSpec((1, tn), lambda j: (0, j))],
        out_specs=pl.BlockSpec((M, tn), lambda j: (0, j)),
        compiler_params=_cparams("arbitrary"),
    )(c, w, b.reshape(1, N))


def _modulated(x, g, scale, shift):
    ms = jnp.mean(x * x, axis=-1, keepdims=True)
    return (x * lax.rsqrt(ms + EPS)) * g * (1.0 + scale) + shift


def _modulate_kernel(x_ref, g_ref, sh_ref, sc_ref, o_ref):
    o_ref[...] = _modulated(x_ref[...], g_ref[...], sc_ref[0], sh_ref[0]).astype(o_ref.dtype)


def _modulate(x2d, S, g, shift, scale):
    T, D = x2d.shape
    tm = _tile(S, (512, 256, 128, 64))
    sh, rows, bidx = _mod_operand(shift, S, tm)
    sc, _, _ = _mod_operand(scale, S, tm)
    mspec = pl.BlockSpec((1, rows, D), lambda i: (bidx(i), 0, 0))
    return pl.pallas_call(
        _modulate_kernel, name="modulate",
        out_shape=jax.ShapeDtypeStruct((T, D), BF16),
        grid=(T // tm,),
        in_specs=[pl.BlockSpec((tm, D), lambda i: (i, 0)),
                  pl.BlockSpec((1, D), lambda i: (0, 0)), mspec, mspec],
        out_specs=pl.BlockSpec((tm, D), lambda i: (i, 0)),
        compiler_params=_cparams("arbitrary"),
    )(x2d, g.reshape(1, D), sh, sc)


def _mm_kernel(a_ref, b_ref, o_ref, *, act):
    acc = jnp.dot(a_ref[...], b_ref[...], preferred_element_type=F32)
    if act == "sigmoid":
        acc = jax.nn.sigmoid(acc)
    o_ref[...] = acc.astype(o_ref.dtype)


def _matmul(a, b, out_dtype, act="none", name="matmul"):
    M, K = a.shape
    N = b.shape[1]
    tm = _tile(M, (1024, 512, 256, 128))
    tn = _tile(N, (512, 384, 256, 128))
    return pl.pallas_call(
        functools.partial(_mm_kernel, act=act), name=name,
        out_shape=jax.ShapeDtypeStruct((M, N), out_dtype),
        grid=(M // tm, N // tn),
        in_specs=[pl.BlockSpec((tm, K), lambda i, j: (i, 0)),
                  pl.BlockSpec((K, tn), lambda i, j: (0, j))],
        out_specs=pl.BlockSpec((tm, tn), lambda i, j: (i, j)),
        compiler_params=_cparams("arbitrary", "arbitrary"),
    )(a, b)


def _attn_post_kernel(raw_ref, qg_ref, kg_ref, ikg_ref,
                      q_ref, k_ref, v_ref, ik_ref, kb_ref, vb_ref, ikb_ref, iq_ref, iw_ref,
                      *, NH, KVH, hd, NIH, DI):
    def headnorm(col, g):
        seg = raw_ref[:, col:col + hd]
        ms = jnp.mean(seg * seg, axis=-1, keepdims=True)
        return seg * lax.rsqrt(ms + EPS) * g

    qg = qg_ref[...] * (LOG2E * hd ** -0.5)
    for h in range(NH):
        q_ref[:, h * hd:(h + 1) * hd] = (headnorm(h * hd, 1.0) * qg).astype(q_ref.dtype)
    off_k = NH * hd
    off_v = off_k + KVH * hd
    for h in range(KVH):
        kn = headnorm(off_k + h * hd, kg_ref[...])
        k_ref[:, h * hd:(h + 1) * hd] = kn
        kb_ref[:, h * hd:(h + 1) * hd] = kn.astype(kb_ref.dtype)
    v = raw_ref[:, off_v:off_v + KVH * hd]
    v_ref[...] = v
    vb_ref[...] = v.astype(vb_ref.dtype)
    off_iq = off_v + KVH * hd
    off_ik = off_iq + NIH * DI
    off_iw = off_ik + DI
    iq_ref[...] = (raw_ref[:, off_iq:off_ik] * (DI ** -0.5)).astype(iq_ref.dtype)
    ikr = raw_ref[:, off_ik:off_iw]
    ikn = ikr * lax.rsqrt(jnp.mean(ikr * ikr, axis=-1, keepdims=True) + EPS) * ikg_ref[...]
    ik_ref[...] = ikn
    ikb_ref[...] = ikn.astype(ikb_ref.dtype)
    iw_ref[...] = raw_ref[:, off_iw:off_iw + NIH] * (NIH ** -0.5)


def _attn_post(raw, q_norm_g, k_norm_g, ik_norm_g, NH, KVH, hd, NIH, DI):
    T, NA = raw.shape
    tr = _tile(T, (256, 128, 64))
    outs = [(NH * hd, BF16), (KVH * hd, F32), (KVH * hd, F32), (DI, F32),
            (KVH * hd, BF16), (KVH * hd, BF16), (DI, BF16), (NIH * DI, BF16), (NIH, F32)]
    return pl.pallas_call(
        functools.partial(_attn_post_kernel, NH=NH, KVH=KVH, hd=hd, NIH=NIH, DI=DI), name="attn_post",
        out_shape=[jax.ShapeDtypeStruct((T, n), dt) for n, dt in outs],
        grid=(T // tr,),
        in_specs=[pl.BlockSpec((tr, NA), lambda i: (i, 0)),
                  pl.BlockSpec((1, hd), lambda i: (0, 0)),
                  pl.BlockSpec((1, hd), lambda i: (0, 0)),
                  pl.BlockSpec((1, DI), lambda i: (0, 0))],
        out_specs=[pl.BlockSpec((tr, n), lambda i: (i, 0)) for n, _ in outs],
        compiler_params=_cparams("arbitrary"),
    )(raw, q_norm_g.reshape(1, hd), k_norm_g.reshape(1, hd), ik_norm_g.reshape(1, DI))


def _conv_kernel(h_ref, b_ref, c_ref, w_ref, prev_ref, y_ref, st_ref, ubuf, *, ts, CW):
    lo = 8 - (CW - 1)

    @pl.when(pl.program_id(1) == 0)
    def _():
        ubuf[lo:8, :] = prev_ref[0]

    @pl.when(pl.program_id(1) > 0)
    def _():
        ubuf[lo:8, :] = ubuf[ts + lo:ts + 8, :]

    ubuf[8:8 + ts, :] = c_ref[0].astype(F32) * h_ref[0].astype(F32)
    y = w_ref[0:1, :] * ubuf[lo:lo + ts, :]
    for j in range(1, CW):
        y = y + w_ref[j:j + 1, :] * ubuf[lo + j:lo + j + ts, :]
    y_ref[0] = (b_ref[0].astype(F32) * y).astype(y_ref.dtype)
    st_ref[0] = ubuf[ts + lo:ts + 8, :]


def _short_conv(hbc, conv_prev, conv_w):
    B, S, C3 = hbc.shape
    CW, C = conv_w.shape
    ts = _tile(S, (512, 256, 128, 64))
    col = lambda k: pl.BlockSpec((1, ts, C), lambda b, s: (b, s, k))
    return pl.pallas_call(
        functools.partial(_conv_kernel, ts=ts, CW=CW), name="short_conv",
        out_shape=[jax.ShapeDtypeStruct((B, S, C), BF16), jax.ShapeDtypeStruct((B, CW - 1, C), F32)],
        grid=(B, S // ts),
        in_specs=[col(0), col(1), col(2),
                  pl.BlockSpec((CW, C), lambda b, s: (0, 0)),
                  pl.BlockSpec((1, CW - 1, C), lambda b, s: (b, 0, 0))],
        out_specs=[pl.BlockSpec((1, ts, C), lambda b, s: (b, s, 0)),
                   pl.BlockSpec((1, CW - 1, C), lambda b, s: (b, 0, 0))],
        scratch_shapes=[pltpu.VMEM((ts + 8, C), F32)],
        compiler_params=_cparams("arbitrary", "arbitrary"),
    )(hbc, hbc, hbc, conv_w, conv_prev)


def _attn_kernel(q_ref, iq_ref, iw_ref, k_ref, v_ref, ik_ref, o_ref, keys_ref, m_sc, l_sc, acc_sc,
                 *, QB, TL, NH, KVH, hd, NIH, DI, topk, causal, L_true, Lp):
    i = pl.program_id(1)
    rep = NH // KVH
    if causal:
        row = i * QB + lax.broadcasted_iota(I32, (QB, 1), 0)
        vis = (row // CHUNK + 1) * CHUNK
        nkt = ((i + 1) * QB + TL - 1) // TL
    else:
        vis = jnp.full((QB, 1), L_true, I32)
        nkt = Lp // TL

    def score_tile(t, carry):
        off = pl.multiple_of(t * TL, TL)
        ikt = ik_ref[0, pl.ds(off, TL), :]
        acc = jnp.zeros((QB, TL), F32)
        for h in range(NIH):
            d = lax.dot_general(iq_ref[0, :, h * DI:(h + 1) * DI], ikt, NT_DIMS, preferred_element_type=F32)
            acc = acc + iw_ref[0, :, h:h + 1] * jnp.maximum(d, 0.0)
        bits = pltpu.bitcast(acc, I32)
        key = bits ^ ((bits >> 31) & jnp.int32(0x7FFFFFFF))
        colid = off + lax.broadcasted_iota(I32, (1, TL), 1)
        keys_ref[:, pl.ds(off, TL)] = jnp.where(colid < vis, key, jnp.int32(INT_MIN))
        return carry

    lax.fori_loop(0, nkt, score_tile, 0)

    def count_ge(cand):
        def body(t, c):
            off = pl.multiple_of(t * TL, TL)
            kt = keys_ref[:, pl.ds(off, TL)]
            for s in range(TL // LANES):
                c = c + jnp.where(kt[:, s * LANES:(s + 1) * LANES] >= cand, 1, 0).astype(I32)
            return c
        c = lax.fori_loop(0, nkt, body, jnp.zeros((QB, LANES), I32))
        return jnp.sum(c, axis=1, keepdims=True)

    def bit_step(bi, thr):
        cand = thr + (jnp.int32(1) << (31 - bi))
        return jnp.where(count_ge(cand) >= topk, cand, thr)

    thr = lax.fori_loop(0, 32, bit_step, jnp.full((QB, 1), INT_MIN, I32))
    thr = jnp.maximum(thr, jnp.int32(INT_MIN + 1))

    for g in range(KVH):
        qg = jnp.concatenate([q_ref[0, :, (g * rep + r) * hd:(g * rep + r + 1) * hd] for r in range(rep)], axis=0)
        m_sc[...] = jnp.full(m_sc.shape, NEG, F32)
        l_sc[...] = jnp.zeros(l_sc.shape, F32)
        acc_sc[...] = jnp.zeros(acc_sc.shape, F32)

        def kv_tile(t, carry):
            off = pl.multiple_of(t * TL, TL)
            kt = k_ref[0, pl.ds(off, TL), g * hd:(g + 1) * hd]
            vt = v_ref[0, pl.ds(off, TL), g * hd:(g + 1) * hd]
            s = lax.dot_general(qg, kt, NT_DIMS, preferred_element_type=F32)
            bias = jnp.where(keys_ref[:, pl.ds(off, TL)] >= thr, 0.0, NEG)
            s = (s.reshape(rep, QB, TL) + bias[None]).reshape(rep * QB, TL)
            m_old = m_sc[...]
            m_new = jnp.maximum(m_old, jnp.max(s, axis=-1, keepdims=True))
            alpha = jnp.exp2(m_old - m_new)
            p = jnp.exp2(s - m_new)
            l_sc[...] = alpha * l_sc[...] + jnp.sum(p, axis=-1, keepdims=True)
            acc_sc[...] = alpha * acc_sc[...] + jnp.dot(p.astype(vt.dtype), vt, preferred_element_type=F32)
            m_sc[...] = m_new
            return carry

        lax.fori_loop(0, nkt, kv_tile, 0)
        o = acc_sc[...] / l_sc[...]
        for r in range(rep):
            o_ref[0, :, (g * rep + r) * hd:(g * rep + r + 1) * hd] = o[r * QB:(r + 1) * QB].astype(o_ref.dtype)


def _attention_rows(q, iq, iw, k_all, v_all, ik_all, *, NH, KVH, hd, NIH, DI, topk, causal, L_true):
    B, S, _ = q.shape
    Lp = k_all.shape[1]
    QB = _tile(S, (128, 64))
    TL = _tile(Lp, (512, 384, 256, 128))
    rep = NH // KVH
    qspec = lambda n: pl.BlockSpec((1, QB, n), lambda b, i: (b, i, 0))
    kspec = lambda n: pl.BlockSpec((1, Lp, n), lambda b, i: (b, 0, 0))
    return pl.pallas_call(
        functools.partial(_attn_kernel, QB=QB, TL=TL, NH=NH, KVH=KVH, hd=hd, NIH=NIH, DI=DI,
                          topk=topk, causal=causal, L_true=L_true, Lp=Lp), name="sparse_attn",
        out_shape=jax.ShapeDtypeStruct((B, S, NH * hd), BF16),
        grid=(B, S // QB),
        in_specs=[qspec(NH * hd), qspec(NIH * DI), qspec(NIH), kspec(KVH * hd), kspec(KVH * hd), kspec(DI)],
        out_specs=qspec(NH * hd),
        scratch_shapes=[pltpu.VMEM((QB, Lp), I32), pltpu.VMEM((rep * QB, 1), F32),
                        pltpu.VMEM((rep * QB, 1), F32), pltpu.VMEM((rep * QB, hd), F32)],
        compiler_params=_cparams("arbitrary", "arbitrary"),
    )(q, iq, iw, k_all, v_all, ik_all)


def _attn_cols_kernel(q_ref, iq_ref, iwt_ref, k_ref, vt_ref, ik_ref, o_ref, keys_ref, bias_sc, m_sc, l_sc, acc_sc,
                      *, QB, TL, NH, KVH, hd, NIH, DI, topk, causal, L_true, Lp):
    i = pl.program_id(1)
    rep = NH // KVH
    if causal:
        qrow = i * QB + lax.broadcasted_iota(I32, (1, QB), 1)
        vis = (qrow // CHUNK + 1) * CHUNK
        nkt = ((i + 1) * QB + TL - 1) // TL
    else:
        vis = jnp.full((1, QB), L_true, I32)
        nkt = Lp // TL

    def score_tile(t, carry):
        off = pl.multiple_of(t * TL, TL)
        ikt = ik_ref[0, pl.ds(off, TL), :]
        acc = jnp.zeros((TL, QB), F32)
        for h in range(NIH):
            d = lax.dot_general(ikt, iq_ref[0, :, h * DI:(h + 1) * DI], NT_DIMS, preferred_element_type=F32)
            acc = acc + iwt_ref[0, h:h + 1, :] * jnp.maximum(d, 0.0)
        bits = pltpu.bitcast(acc, I32)
        key = bits ^ ((bits >> 31) & jnp.int32(0x7FFFFFFF))
        kid = off + lax.broadcasted_iota(I32, (TL, 1), 0)
        keys_ref[pl.ds(off, TL), :] = jnp.where(kid < vis, key, jnp.int32(INT_MIN))
        return carry

    lax.fori_loop(0, nkt, score_tile, 0)

    def count_ge(cand):
        def body(t, c):
            off = pl.multiple_of(t * TL, TL)
            hit = jnp.where(keys_ref[pl.ds(off, TL), :] >= cand, 1, 0).astype(I32)
            return c + jnp.sum(hit.reshape(TL // 8, 8, QB), axis=0)
        c = lax.fori_loop(0, nkt, body, jnp.zeros((8, QB), I32))
        return jnp.sum(c, axis=0, keepdims=True)

    def bit_step(bi, thr):
        cand = thr + (jnp.int32(1) << (31 - bi))
        return jnp.where(count_ge(cand) >= topk, cand, thr)

    thr = lax.fori_loop(0, 32, bit_step, jnp.full((1, QB), INT_MIN, I32))
    thr = jnp.maximum(thr, jnp.int32(INT_MIN + 1))

    m_sc[...] = jnp.full(m_sc.shape, NEG, F32)
    l_sc[...] = jnp.zeros(l_sc.shape, F32)
    acc_sc[...] = jnp.zeros(acc_sc.shape, F32)

    def kv_tile(t, carry):
        off = pl.multiple_of(t * TL, TL)
        bias_sc[...] = jnp.where(keys_ref[pl.ds(off, TL), :] >= thr, 0.0, NEG)
        for g in range(KVH):
            kt = k_ref[0, pl.ds(off, TL), g * hd:(g + 1) * hd]
            vt = vt_ref[0, g * hd:(g + 1) * hd, pl.ds(off, TL)]
            for r in range(rep):
                h = g * rep + r
                s = lax.dot_general(kt, q_ref[0, :, h * hd:(h + 1) * hd], NT_DIMS,
                                    preferred_element_type=F32) + bias_sc[...]
                m_old = m_sc[h:h + 1, :]
                m_new = jnp.maximum(m_old, jnp.max(s, axis=0, keepdims=True))
                alpha = jnp.exp2(m_old - m_new)
                p = jnp.exp2(s - m_new)
                l_sc[h:h + 1, :] = alpha * l_sc[h:h + 1, :] + jnp.sum(p, axis=0, keepdims=True)
                acc_sc[h] = alpha * acc_sc[h] + jnp.dot(vt, p.astype(vt.dtype), preferred_element_type=F32)
                m_sc[h:h + 1, :] = m_new
        return carry

    lax.fori_loop(0, nkt, kv_tile, 0)
    for h in range(NH):
        o = acc_sc[h] / l_sc[h:h + 1, :]
        o_ref[0, :, h * hd:(h + 1) * hd] = o.T.astype(o_ref.dtype)


def _attention_cols(q, iq, iw, k_all, v_all, ik_all, *, NH, KVH, hd, NIH, DI, topk, causal, L_true):
    B, S, _ = q.shape
    Lp = k_all.shape[1]
    QB = LANES
    TL = _tile(Lp, (256, 128))
    qspec = lambda n: pl.BlockSpec((1, QB, n), lambda b, i: (b, i, 0))
    kspec = lambda n: pl.BlockSpec((1, Lp, n), lambda b, i: (b, 0, 0))
    return pl.pallas_call(
        functools.partial(_attn_cols_kernel, QB=QB, TL=TL, NH=NH, KVH=KVH, hd=hd, NIH=NIH, DI=DI,
                          topk=topk, causal=causal, L_true=L_true, Lp=Lp), name="sparse_attn_cols",
        out_shape=jax.ShapeDtypeStruct((B, S, NH * hd), BF16),
        grid=(B, S // QB),
        in_specs=[qspec(NH * hd), qspec(NIH * DI),
                  pl.BlockSpec((1, NIH, QB), lambda b, i: (b, 0, i)),
                  kspec(KVH * hd),
                  pl.BlockSpec((1, KVH * hd, Lp), lambda b, i: (b, 0, 0)),
                  kspec(DI)],
        out_specs=qspec(NH * hd),
        scratch_shapes=[pltpu.VMEM((Lp, QB), I32), pltpu.VMEM((TL, QB), F32), pltpu.VMEM((NH, QB), F32),
                        pltpu.VMEM((NH, QB), F32), pltpu.VMEM((NH, hd, QB), F32)],
        compiler_params=_cparams("arbitrary", "arbitrary"),
    )(q, iq, jnp.swapaxes(iw, 1, 2), k_all, jnp.swapaxes(v_all, 1, 2), ik_all)


def _attention(q, *args, **kw):
    return (_attention_cols if q.shape[1] % LANES == 0 else _attention_rows)(q, *args, **kw)


def _merge_kernel(cy_ref, at_ref, wc_ref, wa_ref, gc_ref, ga_ref, o_ref):
    a = jnp.dot(cy_ref[...], wc_ref[...], preferred_element_type=F32)
    b = jnp.dot(at_ref[...], wa_ref[...], preferred_element_type=F32)
    o_ref[...] = (gc_ref[...].astype(F32) * a + ga_ref[...].astype(F32) * b).astype(o_ref.dtype)


def _merge(conv_y, attn, w_conv_out, w_attn_out, gates):
    T, C = conv_y.shape
    A = attn.shape[1]
    D = w_conv_out.shape[1]
    tm = _tile(T, (1024, 512, 256, 128))
    tn = _tile(D, (512, 256, 128))
    nj = D // tn
    return pl.pallas_call(
        _merge_kernel, name="merge",
        out_shape=jax.ShapeDtypeStruct((T, D), BF16),
        grid=(T // tm, nj),
        in_specs=[pl.BlockSpec((tm, C), lambda i, j: (i, 0)),
                  pl.BlockSpec((tm, A), lambda i, j: (i, 0)),
                  pl.BlockSpec((C, tn), lambda i, j: (0, j)),
                  pl.BlockSpec((A, tn), lambda i, j: (0, j)),
                  pl.BlockSpec((tm, tn), lambda i, j: (i, j)),
                  pl.BlockSpec((tm, tn), lambda i, j: (i, j + nj))],
        out_specs=pl.BlockSpec((tm, tn), lambda i, j: (i, j)),
        compiler_params=_cparams("arbitrary", "arbitrary"),
    )(conv_y, attn, w_conv_out, w_attn_out, gates, gates)


def _outproj_kernel(a_ref, w_ref, x_ref, g_ref, o_ref):
    acc = jnp.dot(a_ref[...], w_ref[...], preferred_element_type=F32)
    o_ref[...] = x_ref[...] + g_ref[0] * acc


def _outproj(merged, w_out, x2d, S, g1):
    T, D = x2d.shape
    K = merged.shape[1]
    tm = _tile(T, (1024, 512, 256, 128))
    tn = _tile(D, (512, 256, 128))
    g, rows, bidx = _mod_operand(g1, S, tm)
    return pl.pallas_call(
        _outproj_kernel, name="outproj",
        out_shape=jax.ShapeDtypeStruct((T, D), F32),
        grid=(T // tm, D // tn),
        in_specs=[pl.BlockSpec((tm, K), lambda i, j: (i, 0)),
                  pl.BlockSpec((K, tn), lambda i, j: (0, j)),
                  pl.BlockSpec((tm, tn), lambda i, j: (i, j)),
                  pl.BlockSpec((1, rows, tn), lambda i, j: (bidx(i), 0, j))],
        out_specs=pl.BlockSpec((tm, tn), lambda i, j: (i, j)),
        compiler_params=_cparams("arbitrary", "arbitrary"),
    )(merged, w_out, x2d, g)


def _router_kernel(x_ref, g_ref, sh_ref, sc_ref, wr_ref, br_ref, xm_ref, eid_ref, ew_ref, *, NG, NE):
    xm = _modulated(x_ref[...], g_ref[...], sc_ref[0], sh_ref[0])
    xm_ref[...] = xm
    logits = jnp.dot(xm, wr_ref[...], preferred_element_type=F32,
                     precision=lax.Precision.HIGHEST) + br_ref[...]
    tm = logits.shape[0]
    epg = NE // NG
    lane = lax.broadcasted_iota(I32, (tm, LANES), 1)
    ninf = -jnp.inf

    def first_argmax(vals):
        mx = jnp.max(vals, axis=1, keepdims=True)
        return mx, jnp.min(jnp.where(vals == mx, lane, LANES), axis=1, keepdims=True)

    gmask = lane < NG
    gmax, g_sel = first_argmax(jnp.where(gmask, logits, ninf))
    g_w = 1.0 / jnp.sum(jnp.where(gmask, jnp.exp(logits - gmax), 0.0), axis=1, keepdims=True)
    emask = (lane >= NG) & (lane < NG + NE) & ((lane - NG) // epg == g_sel)
    el = jnp.where(emask, logits, ninf)
    m1, i1 = first_argmax(el)
    m2, i2 = first_argmax(jnp.where(lane == i1, ninf, el))
    r = jnp.exp(m2 - m1)
    w1 = g_w / (1.0 + r)
    w2 = w1 * r
    eid_ref[...] = jnp.where(lane == 0, i1 - NG, jnp.where(lane == 1, i2 - NG, 0))
    ew_ref[...] = jnp.where(lane == 0, w1, jnp.where(lane == 1, w2, 0.0))


def _router(x2d, S, g, shift, scale, wr, br, NG, NE):
    T, D = x2d.shape
    tm = _tile(S, (256, 128, 64))
    sh, rows, bidx = _mod_operand(shift, S, tm)
    sc, _, _ = _mod_operand(scale, S, tm)
    mspec = pl.BlockSpec((1, rows, D), lambda i: (bidx(i), 0, 0))
    rowspec = lambda n: pl.BlockSpec((tm, n), lambda i: (i, 0))
    return pl.pallas_call(
        functools.partial(_router_kernel, NG=NG, NE=NE), name="router",
        out_shape=[jax.ShapeDtypeStruct((T, D), F32), jax.ShapeDtypeStruct((T, LANES), I32),
                   jax.ShapeDtypeStruct((T, LANES), F32)],
        grid=(T // tm,),
        in_specs=[rowspec(D), pl.BlockSpec((1, D), lambda i: (0, 0)), mspec, mspec,
                  pl.BlockSpec((D, LANES), lambda i: (0, 0)), pl.BlockSpec((1, LANES), lambda i: (0, 0))],
        out_specs=[rowspec(D), rowspec(LANES), rowspec(LANES)],
        compiler_params=_cparams("arbitrary"),
    )(x2d, g.reshape(1, D), sh, sc, wr, br)


def _gather_kernel(tok_ref, nxt_ref, xa_hbm, xb_hbm, o_ref, buf, sem, *, R, Ta):
    i = pl.program_id(0)
    slot = i % 2

    def row_copy(s, r, src, t):
        return pltpu.make_async_copy(src.at[pl.ds(t, 1)], buf.at[s, pl.ds(r, 1)], sem.at[s])

    def start_rows(tref, s):
        def body(r, carry):
            t = tref[0, 0, r]

            @pl.when(t < Ta)
            def _():
                row_copy(s, r, xa_hbm, t).start()

            @pl.when(t >= Ta)
            def _():
                row_copy(s, r, xb_hbm, t - Ta).start()
            return carry
        lax.fori_loop(0, R, body, 0)

    @pl.when((i == 0) & (tok_ref[0, 0, 0] >= 0))
    def _():
        start_rows(tok_ref, 0)

    @pl.when((i + 1 < pl.num_programs(0)) & (nxt_ref[0, 0, 0] >= 0))
    def _():
        start_rows(nxt_ref, 1 - slot)

    @pl.when(tok_ref[0, 0, 0] >= 0)
    def _():
        def wait(r, carry):
            row_copy(slot, r, xa_hbm, 0).wait()
            return carry
        lax.fori_loop(0, R, wait, 0)
        o_ref[...] = buf[slot].astype(o_ref.dtype)

    @pl.when(tok_ref[0, 0, 0] < 0)
    def _():
        o_ref[...] = jnp.zeros(o_ref.shape, o_ref.dtype)


def _moe_gather(tok_buf, xa, xb):
    P = tok_buf.shape[0]
    D = xa.shape[1]
    R = _tile(P, (256, 128, 64))
    n = P // R
    toks = tok_buf.reshape(n, 1, R)
    return pl.pallas_call(
        functools.partial(_gather_kernel, R=R, Ta=xa.shape[0]), name="moe_gather",
        out_shape=jax.ShapeDtypeStruct((P, D), BF16),
        grid=(n,),
        in_specs=[pl.BlockSpec((1, 1, R), lambda i: (i, 0, 0), memory_space=pltpu.SMEM),
                  pl.BlockSpec((1, 1, R), lambda i: (jnp.minimum(i + 1, n - 1), 0, 0), memory_space=pltpu.SMEM),
                  pl.BlockSpec(memory_space=pl.ANY), pl.BlockSpec(memory_space=pl.ANY)],
        out_specs=pl.BlockSpec((R, D), lambda i: (i, 0)),
        scratch_shapes=[pltpu.VMEM((2, R, D), xa.dtype), pltpu.SemaphoreType.DMA((2,))],
        compiler_params=_cparams("arbitrary"),
    )(toks, toks, xa, xb)


def _expert_kernel(be_ref, nu_ref, xs_ref, wg_ref, wu_ref, wd_ref, o_ref):
    i = pl.program_id(0)
    j = pl.program_id(1)

    @pl.when(j == 0)
    def _():
        o_ref[...] = jnp.zeros(o_ref.shape, o_ref.dtype)

    @pl.when(i < nu_ref[0])
    def _():
        x = xs_ref[...]
        gate = jnp.dot(x, wg_ref[0].astype(BF16), preferred_element_type=F32)
        up = jnp.dot(x, wu_ref[0].astype(BF16), preferred_element_type=F32)
        hidden = (gate * jax.nn.sigmoid(gate)) * up
        o_ref[...] += jnp.dot(hidden.astype(BF16), wd_ref[0].astype(BF16), preferred_element_type=F32)


def _moe_experts(block_e, nused, xs, w_gate, w_up, w_down, BM):
    P, D = xs.shape
    NE, _, DE = w_gate.shape
    tf = _tile(DE, (256, 128))
    nf = DE // tf
    row = lambda i, nu: jnp.minimum(i, nu[0] - 1)
    fj = lambda i, j, nu: jnp.where(i < nu[0], j, nf - 1)
    grid_spec = pltpu.PrefetchScalarGridSpec(
        num_scalar_prefetch=2,
        grid=(P // BM, nf),
        in_specs=[pl.BlockSpec((BM, D), lambda i, j, be, nu: (row(i, nu), 0)),
                  pl.BlockSpec((1, D, tf), lambda i, j, be, nu: (be[row(i, nu)], 0, fj(i, j, nu))),
                  pl.BlockSpec((1, D, tf), lambda i, j, be, nu: (be[row(i, nu)], 0, fj(i, j, nu))),
                  pl.BlockSpec((1, tf, D), lambda i, j, be, nu: (be[row(i, nu)], fj(i, j, nu), 0))],
        out_specs=pl.BlockSpec((BM, D), lambda i, j, be, nu: (i, 0)),
    )
    return pl.pallas_call(
        _expert_kernel, name="moe_experts", grid_spec=grid_spec,
        out_shape=jax.ShapeDtypeStruct((P, D), F32),
        compiler_params=_cparams("arbitrary", "arbitrary"),
    )(block_e, nused, xs, w_gate, w_up, w_down)


def _combine_kernel(pos_ref, x_ref, g_ref, ew_ref, ys_hbm, o_ref, buf, sem, *, tc):
    def row_copy(r, jj, p):
        return pltpu.make_async_copy(ys_hbm.at[pl.ds(p, 1)], buf.at[jj, pl.ds(r, 1)], sem)

    def start(r, carry):
        for jj in range(TOPK_EXPERTS):
            row_copy(r, jj, pos_ref[0, 0, TOPK_EXPERTS * r + jj]).start()
        return carry

    lax.fori_loop(0, tc, start, 0)

    def wait(r, carry):
        for jj in range(TOPK_EXPERTS):
            row_copy(r, jj, 0).wait()
        return carry

    lax.fori_loop(0, tc, wait, 0)
    moe = ew_ref[:, 0:1] * buf[0] + ew_ref[:, 1:2] * buf[1]
    o_ref[...] = x_ref[...] + g_ref[0] * moe


def _moe_combine(pos, x2d, S, g2, ew, ys):
    T, D = x2d.shape
    tc = _tile(T, (256, 128, 64))
    g, rows, bidx = _mod_operand(g2, S, tc)
    return pl.pallas_call(
        functools.partial(_combine_kernel, tc=tc), name="moe_combine",
        out_shape=jax.ShapeDtypeStruct((T, D), F32),
        grid=(T // tc,),
        in_specs=[pl.BlockSpec((1, 1, TOPK_EXPERTS * tc), lambda i: (i, 0, 0), memory_space=pltpu.SMEM),
                  pl.BlockSpec((tc, D), lambda i: (i, 0)),
                  pl.BlockSpec((1, rows, D), lambda i: (bidx(i), 0, 0)),
                  pl.BlockSpec((tc, LANES), lambda i: (i, 0)),
                  pl.BlockSpec(memory_space=pl.ANY)],
        out_specs=pl.BlockSpec((tc, D), lambda i: (i, 0)),
        scratch_shapes=[pltpu.VMEM((TOPK_EXPERTS, tc, D), F32), pltpu.SemaphoreType.DMA(())],
        compiler_params=_cparams("arbitrary"),
    )(pos.reshape(T // tc, 1, TOPK_EXPERTS * tc), x2d, g, ew, ys)


def _route_meta(eid, NE, BM):
    n_slots = eid.shape[0] * TOPK_EXPERTS
    slot_e = eid.reshape(-1)
    order = jnp.argsort(slot_e, stable=True).astype(I32)
    se = slot_e[order]
    experts = jnp.arange(NE, dtype=I32)
    start = jnp.searchsorted(se, experts, side="left").astype(I32)
    counts = jnp.searchsorted(se, experts, side="right").astype(I32) - start
    padded = (counts + BM - 1) // BM * BM
    pend = jnp.cumsum(padded)
    pstart = pend - padded
    P = -(-n_slots // BM) * BM + NE * BM
    nblk = P // BM
    block_e = jnp.minimum(jnp.searchsorted(pend, jnp.arange(nblk, dtype=I32) * BM, side="right"), NE - 1).astype(I32)
    nused = (pend[-1] // BM).astype(I32).reshape(1)
    prow = jnp.arange(P, dtype=I32)
    pe = block_e[prow // BM]
    rank = prow - pstart[pe]
    valid = rank < counts[pe]
    src = jnp.clip(start[pe] + rank, 0, n_slots - 1)
    tok_buf = jnp.where(valid, order[src] // TOPK_EXPERTS, 0)
    tok_buf = jnp.where(prow < pend[-1], tok_buf, -1).astype(I32)
    dest_sorted = pstart[se] + (jnp.arange(n_slots, dtype=I32) - start[se])
    pos = dest_sorted[jnp.argsort(order).astype(I32)].astype(I32)
    return tok_buf, pos, block_e, nused


def _layer_front(x, mod, conv_prev, past, W, dims):
    B, S, D = x.shape
    T = B * S
    NH, KVH, hd, NIH, DI, NG, NE = dims
    sh1, sc1, g1, sh2, sc2, g2 = jnp.split(mod, 6, axis=-1)
    x2d = x.reshape(T, D)
    xm = _modulate(x2d, S, W["norm1_g"], sh1, sc1)
    hbc = _matmul(xm, W["w_in_conv"], BF16, name="inproj_conv")
    raw = _matmul(xm, W["w_in_attn"], F32, name="inproj_attn")
    gates = _matmul(xm, W["w_in_gate"], BF16, act="sigmoid", name="inproj_gate")
    q, k, v, ik, kb, vb, ikb, iq, iw = _attn_post(raw, W["q_norm_g"], W["k_norm_g"], W["ik_norm_g"],
                                                  NH, KVH, hd, NIH, DI)
    C = W["conv_w"].shape[1]
    conv_y, conv_state = _short_conv(hbc.reshape(B, S, 3 * C), conv_prev, W["conv_w"])
    r3 = lambda a: a.reshape(B, S, a.shape[-1])
    if past is None:
        k_all, v_all, ik_all = r3(kb), r3(vb), r3(ikb)
        L_true, causal = S, True
    else:
        k_past, v_past, ik_past = past
        Pn = k_past.shape[1]
        L_true, causal = Pn + S, False
        Lp = -(-L_true // LANES) * LANES
        cat = lambda old, new: jnp.pad(
            jnp.concatenate([old.reshape(B, Pn, -1).astype(BF16), r3(new)], axis=1),
            ((0, 0), (0, Lp - L_true), (0, 0)))
        k_all, v_all, ik_all = cat(k_past, kb), cat(v_past, vb), cat(ik_past, ikb)
    attn = _attention(r3(q), r3(iq), r3(iw), k_all, v_all, ik_all, NH=NH, KVH=KVH, hd=hd, NIH=NIH, DI=DI,
                      topk=min(TOPK_MAX, L_true // 4), causal=causal, L_true=L_true)
    merged = _merge(conv_y.reshape(T, C), attn.reshape(T, NH * hd), W["w_conv_out"], W["w_attn_out"], gates)
    x_mid = _outproj(merged, W["w_out"], x2d, S, g1)
    xm2, eid, ew = _router(x_mid, S, W["norm2_g"], sh2, sc2, W["w_router"], W["b_router"], NG, NE)
    outs = (k.reshape(B, S, KVH, hd), v.reshape(B, S, KVH, hd), ik.reshape(B, S, DI), conv_state)
    return x_mid, xm2, eid[:, :TOPK_EXPERTS], ew, g2, outs


def kernel(x_prompt, x_sample, cache_k, cache_v, cache_idx_k, state_conv, c_prompt, c_sample, norm1_g, norm2_g, w_ada, b_ada, w_in, conv_w, q_norm_g, k_norm_g, ik_norm_g, w_conv_out, w_attn_out, w_out, w_router_group, b_router_group, w_router_expert, b_router_expert, w_gate, w_up, w_down):
    Bp, Sp, D = x_prompt.shape
    Bs, Ss, _ = x_sample.shape
    CW, C = conv_w.shape
    hd = q_norm_g.shape[0]
    DI = ik_norm_g.shape[0]
    KVH = cache_k.shape[2]
    NH = w_attn_out.shape[0] // hd
    NG = b_router_group.shape[0]
    NE = b_router_expert.shape[0]
    off_q = 3 * C
    off_iq = off_q + NH * hd + 2 * KVH * hd
    off_gate = w_in.shape[1] - 2 * D
    NIH = (off_gate - off_iq - DI) // (DI + 1)
    dims = (NH, KVH, hd, NIH, DI, NG, NE)

    na = off_gate - off_q
    na_pad = -(-na // LANES) * LANES
    rpad = LANES - NG - NE
    W = {
        "norm1_g": norm1_g, "norm2_g": norm2_g, "conv_w": conv_w,
        "q_norm_g": q_norm_g, "k_norm_g": k_norm_g, "ik_norm_g": ik_norm_g,
        "w_in_conv": w_in[:, :off_q].astype(BF16),
        "w_in_attn": jnp.pad(w_in[:, off_q:off_gate], ((0, 0), (0, na_pad - na))).astype(BF16),
        "w_in_gate": w_in[:, off_gate:].astype(BF16),
        "w_conv_out": w_conv_out.astype(BF16), "w_attn_out": w_attn_out.astype(BF16),
        "w_out": w_out.astype(BF16),
        "w_router": jnp.pad(jnp.concatenate([w_router_group, w_router_expert], axis=1), ((0, 0), (0, rpad))),
        "b_router": jnp.pad(jnp.concatenate([b_router_group, b_router_expert]), (0, rpad)).reshape(1, LANES),
    }

    nb = Bp + Bs
    c_all = jnp.pad(jnp.concatenate([c_prompt, c_sample], axis=0), ((0, -nb % 8), (0, 0)))
    mod = _ada(c_all, w_ada, b_ada)

    conv_zero = jnp.zeros((Bp, CW - 1, C), F32)
    xmid_p, xm2_p, eid_p, ew_p, g2_p, outs_p = _layer_front(x_prompt, mod[:Bp], conv_zero, None, W, dims)
    xmid_s, xm2_s, eid_s, ew_s, g2_s, outs_s = _layer_front(
        x_sample, mod[Bp:nb], state_conv, (cache_k, cache_v, cache_idx_k), W, dims)

    Tp = Bp * Sp
    tok_buf, pos, block_e, nused = _route_meta(jnp.concatenate([eid_p, eid_s], axis=0), NE, MOE_BM)
    xs = _moe_gather(tok_buf, xm2_p, xm2_s)
    ys = _moe_experts(block_e, nused, xs, w_gate, w_up, w_down, MOE_BM)
    y_p = _moe_combine(pos[:TOPK_EXPERTS * Tp], xmid_p, Sp, g2_p, ew_p, ys).reshape(Bp, Sp, D)
    y_s = _moe_combine(pos[TOPK_EXPERTS * Tp:], xmid_s, Ss, g2_s, ew_s, ys).reshape(Bs, Ss, D)

    k_p, v_p, ik_p, conv_p = outs_p
    k_s, v_s, ik_s, conv_s = outs_s
    return (y_p, y_s, k_p, v_p, ik_p, conv_p, k_s, v_s, ik_s, conv_s)
```

```python
import functools

import jax
import jax.numpy as jnp
from jax import lax
from jax.experimental import pallas as pl
from jax.experimental.pallas import tpu as pltpu

F32 = jnp.float32
BF16 = jnp.bfloat16
I32 = jnp.int32

CHUNK = 64
EPS = 1e-6
TOPK_MAX = 256
TOPK_EXPERTS = 2
LANES = 128
VMEM_LIMIT = 56 * 1024 * 1024
INT_MIN = -2 ** 31
NEG = -1e30
LOG2E = 1.4426950408889634
MOE_BM = 512
ATTN_KEY_TILE = 256
IDX_HEADS_PER_DOT = 16
NT_DIMS = (((1,), (1,)), ((), ()))


def _tile(n, prefs):
    for t in prefs:
        if n % t == 0:
            return t
    return n


def _cparams(*sem):
    return pltpu.CompilerParams(dimension_semantics=sem, vmem_limit_bytes=VMEM_LIMIT)


def _mod_operand(m, S, tm):
    B, D = m.shape
    if S % tm == 0:
        return m.reshape(B, 1, D), 1, (lambda i: (i * tm) // S)
    assert tm % S == 0
    full = jnp.broadcast_to(m[:, None, :], (B, S, D)).reshape((B * S) // tm, tm, D)
    return full, tm, (lambda i: i)


def _ada_kernel(c_ref, w_ref, b_ref, o_ref):
    o_ref[...] = jnp.dot(c_ref[...], w_ref[...], preferred_element_type=F32,
                         precision=lax.Precision.HIGHEST) + b_ref[...]


def _ada(c, w, b):
    M, D = c.shape
    N = w.shape[1]
    tn = _tile(N, (512, 256, 128))
    return pl.pallas_call(
        _ada_kernel, name="ada",
        out_shape=jax.ShapeDtypeStruct((M, N), F32),
        grid=(N // tn,),
        in_specs=[pl.BlockSpec((M, D), lambda j: (0, 0)),
                  pl.BlockSpec((D, tn), lambda j: (0, j)),
                  pl.BlockSpec((1, tn), lambda j: (0, j))],
        out_specs=pl.BlockSpec((M, tn), lambda j: (0, j)),
        compiler_params=_cparams("arbitrary"),
    )(c, w, b.reshape(1, N))


def _modulated(x, g, scale, shift):
    ms = jnp.mean(x * x, axis=-1, keepdims=True)
    return (x * lax.rsqrt(ms + EPS)) * g * (1.0 + scale) + shift


def _modulate_kernel(x_ref, g_ref, sh_ref, sc_ref, o_ref):
    o_ref[...] = _modulated(x_ref[...], g_ref[...], sc_ref[0], sh_ref[0]).astype(o_ref.dtype)


def _modulate(x2d, S, g, shift, scale):
    T, D = x2d.shape
    tm = _tile(S, (512, 256, 128, 64))
    sh, rows, bidx = _mod_operand(shift, S, tm)
    sc, _, _ = _mod_operand(scale, S, tm)
    mspec = pl.BlockSpec((1, rows, D), lambda i: (bidx(i), 0, 0))
    return pl.pallas_call(
        _modulate_kernel, name="modulate",
        out_shape=jax.ShapeDtypeStruct((T, D), BF16),
        grid=(T // tm,),
        in_specs=[pl.BlockSpec((tm, D), lambda i: (i, 0)),
                  pl.BlockSpec((1, D), lambda i: (0, 0)), mspec, mspec],
        out_specs=pl.BlockSpec((tm, D), lambda i: (i, 0)),
        compiler_params=_cparams("arbitrary"),
    )(x2d, g.reshape(1, D), sh, sc)


def _mm_kernel(a_ref, b_ref, o_ref, *, act):
    acc = jnp.dot(a_ref[...], b_ref[...], preferred_element_type=F32)
    if act == "sigmoid":
        acc = jax.nn.sigmoid(acc)
    o_ref[...] = acc.astype(o_ref.dtype)


def _matmul(a, b, out_dtype, act="none", name="matmul"):
    M, K = a.shape
    N = b.shape[1]
    tm = _tile(M, (1024, 512, 256, 128))
    tn = _tile(N, (512, 384, 256, 128))
    return pl.pallas_call(
        functools.partial(_mm_kernel, act=act), name=name,
        out_shape=jax.ShapeDtypeStruct((M, N), out_dtype),
        grid=(M // tm, N // tn),
        in_specs=[pl.BlockSpec((tm, K), lambda i, j: (i, 0)),
                  pl.BlockSpec((K, tn), lambda i, j: (0, j))],
        out_specs=pl.BlockSpec((tm, tn), lambda i, j: (i, j)),
        compiler_params=_cparams("arbitrary", "arbitrary"),
    )(a, b)


def _attn_post_kernel(raw_ref, qg_ref, kg_ref, ikg_ref,
                      q_ref, k_ref, v_ref, ik_ref, kb_ref, vb_ref, ikb_ref, iq_ref, iw_ref,
                      *, NH, KVH, hd, NIH, DI):
    def headnorm(col, g):
        seg = raw_ref[:, col:col + hd]
        ms = jnp.mean(seg * seg, axis=-1, keepdims=True)
        return seg * lax.rsqrt(ms + EPS) * g

    qg = qg_ref[...] * (LOG2E * hd ** -0.5)
    for h in range(NH):
        q_ref[:, h * hd:(h + 1) * hd] = (headnorm(h * hd, 1.0) * qg).astype(q_ref.dtype)
    off_k = NH * hd
    off_v = off_k + KVH * hd
    for h in range(KVH):
        kn = headnorm(off_k + h * hd, kg_ref[...])
        k_ref[:, h * hd:(h + 1) * hd] = kn
        kb_ref[:, h * hd:(h + 1) * hd] = kn.astype(kb_ref.dtype)
    v = raw_ref[:, off_v:off_v + KVH * hd]
    v_ref[...] = v
    vb_ref[...] = v.astype(vb_ref.dtype)
    off_iq = off_v + KVH * hd
    off_ik = off_iq + NIH * DI
    off_iw = off_ik + DI
    iq_ref[...] = (raw_ref[:, off_iq:off_ik] * (DI ** -0.5)).astype(iq_ref.dtype)
    ikr = raw_ref[:, off_ik:off_iw]
    ikn = ikr * lax.rsqrt(jnp.mean(ikr * ikr, axis=-1, keepdims=True) + EPS) * ikg_ref[...]
    ik_ref[...] = ikn
    ikb_ref[...] = ikn.astype(ikb_ref.dtype)
    iw_ref[...] = raw_ref[:, off_iw:off_iw + NIH] * (NIH ** -0.5)


def _attn_post(raw, q_norm_g, k_norm_g, ik_norm_g, NH, KVH, hd, NIH, DI):
    T, NA = raw.shape
    tr = _tile(T, (256, 128, 64))
    outs = [(NH * hd, BF16), (KVH * hd, F32), (KVH * hd, F32), (DI, F32),
            (KVH * hd, BF16), (KVH * hd, BF16), (DI, BF16), (NIH * DI, BF16), (NIH, F32)]
    return pl.pallas_call(
        functools.partial(_attn_post_kernel, NH=NH, KVH=KVH, hd=hd, NIH=NIH, DI=DI), name="attn_post",
        out_shape=[jax.ShapeDtypeStruct((T, n), dt) for n, dt in outs],
        grid=(T // tr,),
        in_specs=[pl.BlockSpec((tr, NA), lambda i: (i, 0)),
                  pl.BlockSpec((1, hd), lambda i: (0, 0)),
                  pl.BlockSpec((1, hd), lambda i: (0, 0)),
                  pl.BlockSpec((1, DI), lambda i: (0, 0))],
        out_specs=[pl.BlockSpec((tr, n), lambda i: (i, 0)) for n, _ in outs],
        compiler_params=_cparams("arbitrary"),
    )(raw, q_norm_g.reshape(1, hd), k_norm_g.reshape(1, hd), ik_norm_g.reshape(1, DI))


def _conv_kernel(h_ref, b_ref, c_ref, w_ref, prev_ref, y_ref, st_ref, ubuf, *, ts, CW):
    lo = 8 - (CW - 1)

    @pl.when(pl.program_id(1) == 0)
    def _():
        ubuf[lo:8, :] = prev_ref[0]

    @pl.when(pl.program_id(1) > 0)
    def _():
        ubuf[lo:8, :] = ubuf[ts + lo:ts + 8, :]

    ubuf[8:8 + ts, :] = c_ref[0].astype(F32) * h_ref[0].astype(F32)
    y = w_ref[0:1, :] * ubuf[lo:lo + ts, :]
    for j in range(1, CW):
        y = y + w_ref[j:j + 1, :] * ubuf[lo + j:lo + j + ts, :]
    y_ref[0] = (b_ref[0].astype(F32) * y).astype(y_ref.dtype)
    st_ref[0] = ubuf[ts + lo:ts + 8, :]


def _short_conv(hbc, conv_prev, conv_w):
    B, S, C3 = hbc.shape
    CW, C = conv_w.shape
    ts = _tile(S, (512, 256, 128, 64))
    col = lambda k: pl.BlockSpec((1, ts, C), lambda b, s: (b, s, k))
    return pl.pallas_call(
        functools.partial(_conv_kernel, ts=ts, CW=CW), name="short_conv",
        out_shape=[jax.ShapeDtypeStruct((B, S, C), BF16), jax.ShapeDtypeStruct((B, CW - 1, C), F32)],
        grid=(B, S // ts),
        in_specs=[col(0), col(1), col(2),
                  pl.BlockSpec((CW, C), lambda b, s: (0, 0)),
                  pl.BlockSpec((1, CW - 1, C), lambda b, s: (b, 0, 0))],
        out_specs=[pl.BlockSpec((1, ts, C), lambda b, s: (b, s, 0)),
                   pl.BlockSpec((1, CW - 1, C), lambda b, s: (b, 0, 0))],
        scratch_shapes=[pltpu.VMEM((ts + 8, C), F32)],
        compiler_params=_cparams("arbitrary", "arbitrary"),
    )(hbc, hbc, hbc, conv_w, conv_prev)


def _attn_kernel(q_ref, iq_ref, iwt_ref, k_ref, vt_ref, ik_ref, o_ref, keys_ref, bias_sc, m_sc, l_sc, acc_sc,
                 *, QB, TL, NH, KVH, hd, NIH, DI, topk, causal, L_true, Lp):
    i = pl.program_id(1)
    rep = NH // KVH
    if causal:
        qrow = i * QB + lax.broadcasted_iota(I32, (1, QB), 1)
        vis = (qrow // CHUNK + 1) * CHUNK
        nkt = ((i + 1) * QB + TL - 1) // TL
    else:
        vis = jnp.full((1, QB), L_true, I32)
        nkt = Lp // TL

    def score_tile(t, carry):
        off = pl.multiple_of(t * TL, TL)
        ikt = ik_ref[0, pl.ds(off, TL), :]
        acc = jnp.zeros((TL, QB), F32)
        for h0 in range(0, NIH, IDX_HEADS_PER_DOT):
            nh = min(IDX_HEADS_PER_DOT, NIH - h0)
            d = lax.dot_general(ikt, iq_ref[0, 0, h0 * QB:(h0 + nh) * QB, :], NT_DIMS, preferred_element_type=F32)
            for h in range(h0, h0 + nh):
                acc = acc + iwt_ref[0, h:h + 1, :] * jnp.maximum(d[:, (h - h0) * QB:(h - h0 + 1) * QB], 0.0)
        bits = pltpu.bitcast(acc, I32)
        key = bits ^ ((bits >> 31) & jnp.int32(0x7FFFFFFF))
        kid = off + lax.broadcasted_iota(I32, (TL, 1), 0)
        keys_ref[pl.ds(off, TL), :] = jnp.where(kid < vis, key, jnp.int32(INT_MIN))
        return carry

    lax.fori_loop(0, nkt, score_tile, 0)

    def count_ge(cand):
        def body(t, c):
            off = pl.multiple_of(t * TL, TL)
            hit = jnp.where(keys_ref[pl.ds(off, TL), :] >= cand, 1, 0).astype(I32)
            return c + jnp.sum(hit.reshape(TL // 8, 8, QB), axis=0)
        c = lax.fori_loop(0, nkt, body, jnp.zeros((8, QB), I32))
        return jnp.sum(c, axis=0, keepdims=True)

    def bit_step(bi, thr):
        cand = thr + (jnp.int32(1) << (31 - bi))
        return jnp.where(count_ge(cand) >= topk, cand, thr)

    thr = lax.fori_loop(0, 32, bit_step, jnp.full((1, QB), INT_MIN, I32))
    thr = jnp.maximum(thr, jnp.int32(INT_MIN + 1))

    def tie_search(_):
        need = topk - count_ge(thr + 1)

        def count_eq_below(cand):
            def body(t, c):
                off = pl.multiple_of(t * TL, TL)
                kid = off + lax.broadcasted_iota(I32, (TL, 1), 0)
                hit = jnp.where((keys_ref[pl.ds(off, TL), :] == thr) & (kid < cand), 1, 0).astype(I32)
                return c + jnp.sum(hit.reshape(TL // 8, 8, QB), axis=0)
            c = lax.fori_loop(0, nkt, body, jnp.zeros((8, QB), I32))
            return jnp.sum(c, axis=0, keepdims=True)

        def idx_step(bi, lo):
            cand = lo + (jnp.int32(1) << (IDX_BITS - 1 - bi))
            return jnp.where(count_eq_below(cand) < need, cand, lo)

        return lax.fori_loop(0, IDX_BITS, idx_step, jnp.zeros((1, QB), I32))

    IDX_BITS = max(1, Lp.bit_length())
    tie_idx = lax.cond(jnp.max(count_ge(thr)) > topk, tie_search,
                       lambda _: jnp.full((1, QB), Lp, I32), 0)

    m_sc[...] = jnp.full(m_sc.shape, NEG, F32)
    l_sc[...] = jnp.zeros(l_sc.shape, F32)
    acc_sc[...] = jnp.zeros(acc_sc.shape, F32)

    def kv_tile(t, carry):
        off = pl.multiple_of(t * TL, TL)
        kid = off + lax.broadcasted_iota(I32, (TL, 1), 0)
        kt_keys = keys_ref[pl.ds(off, TL), :]
        keep = (kt_keys > thr) | ((kt_keys == thr) & (kid <= tie_idx))
        bias_sc[...] = jnp.where(keep, 0.0, NEG)
        for g in range(KVH):
            kt = k_ref[0, pl.ds(off, TL), g * hd:(g + 1) * hd]
            vt = vt_ref[0, g * hd:(g + 1) * hd, pl.ds(off, TL)]
            for r in range(rep):
                h = g * rep + r
                s = lax.dot_general(kt, q_ref[0, :, h * hd:(h + 1) * hd], NT_DIMS,
                                    preferred_element_type=F32) + bias_sc[...]
                m_old = m_sc[h:h + 1, :]
                m_new = jnp.maximum(m_old, jnp.max(s, axis=0, keepdims=True))
                alpha = jnp.exp2(m_old - m_new)
                p = jnp.exp2(s - m_new)
                l_sc[h:h + 1, :] = alpha * l_sc[h:h + 1, :] + jnp.sum(p, axis=0, keepdims=True)
                acc_sc[h] = alpha * acc_sc[h] + jnp.dot(vt, p.astype(vt.dtype), preferred_element_type=F32)
                m_sc[h:h + 1, :] = m_new
        return carry

    lax.fori_loop(0, nkt, kv_tile, 0)
    for h in range(NH):
        o = acc_sc[h] / l_sc[h:h + 1, :]
        o_ref[0, :, h * hd:(h + 1) * hd] = o.T.astype(o_ref.dtype)


def _attention(q, iq, iw, k_all, v_all, ik_all, *, NH, KVH, hd, NIH, DI, topk, causal, L_true):
    B, S0, _ = q.shape
    QB = LANES
    S = -(-S0 // QB) * QB
    TL = ATTN_KEY_TILE
    Lp = -(-k_all.shape[1] // TL) * TL
    pad_s = lambda a: jnp.pad(a, ((0, 0), (0, S - S0), (0, 0)))
    pad_l = lambda a: jnp.pad(a, ((0, 0), (0, Lp - a.shape[1]), (0, 0)))
    q, iq, iw = pad_s(q), pad_s(iq), pad_s(iw)
    k_all, v_all, ik_all = pad_l(k_all), pad_l(v_all), pad_l(ik_all)
    iq_heads = iq.reshape(B, S // QB, QB, NIH, DI).transpose(0, 1, 3, 2, 4).reshape(B, S // QB, NIH * QB, DI)
    qspec = lambda n: pl.BlockSpec((1, QB, n), lambda b, i: (b, i, 0))
    kspec = lambda n: pl.BlockSpec((1, Lp, n), lambda b, i: (b, 0, 0))
    out = pl.pallas_call(
        functools.partial(_attn_kernel, QB=QB, TL=TL, NH=NH, KVH=KVH, hd=hd, NIH=NIH, DI=DI,
                          topk=topk, causal=causal, L_true=L_true, Lp=Lp), name="sparse_attn",
        out_shape=jax.ShapeDtypeStruct((B, S, NH * hd), BF16),
        grid=(B, S // QB),
        in_specs=[qspec(NH * hd),
                  pl.BlockSpec((1, 1, NIH * QB, DI), lambda b, i: (b, i, 0, 0)),
                  pl.BlockSpec((1, NIH, QB), lambda b, i: (b, 0, i)),
                  kspec(KVH * hd),
                  pl.BlockSpec((1, KVH * hd, Lp), lambda b, i: (b, 0, 0)),
                  kspec(DI)],
        out_specs=qspec(NH * hd),
        scratch_shapes=[pltpu.VMEM((Lp, QB), I32), pltpu.VMEM((TL, QB), F32), pltpu.VMEM((NH, QB), F32),
                        pltpu.VMEM((NH, QB), F32), pltpu.VMEM((NH, hd, QB), F32)],
        compiler_params=_cparams("arbitrary", "arbitrary"),
    )(q, iq_heads, jnp.swapaxes(iw, 1, 2), k_all, jnp.swapaxes(v_all, 1, 2), ik_all)
    return out[:, :S0]


def _merge_kernel(cy_ref, at_ref, wc_ref, wa_ref, gc_ref, ga_ref, o_ref):
    a = jnp.dot(cy_ref[...], wc_ref[...], preferred_element_type=F32)
    b = jnp.dot(at_ref[...], wa_ref[...], preferred_element_type=F32)
    o_ref[...] = (gc_ref[...].astype(F32) * a + ga_ref[...].astype(F32) * b).astype(o_ref.dtype)


def _merge(conv_y, attn, w_conv_out, w_attn_out, gates):
    T, C = conv_y.shape
    A = attn.shape[1]
    D = w_conv_out.shape[1]
    tm = _tile(T, (1024, 512, 256, 128))
    tn = _tile(D, (512, 256, 128))
    nj = D // tn
    return pl.pallas_call(
        _merge_kernel, name="merge",
        out_shape=jax.ShapeDtypeStruct((T, D), BF16),
        grid=(T // tm, nj),
        in_specs=[pl.BlockSpec((tm, C), lambda i, j: (i, 0)),
                  pl.BlockSpec((tm, A), lambda i, j: (i, 0)),
                  pl.BlockSpec((C, tn), lambda i, j: (0, j)),
                  pl.BlockSpec((A, tn), lambda i, j: (0, j)),
                  pl.BlockSpec((tm, tn), lambda i, j: (i, j)),
                  pl.BlockSpec((tm, tn), lambda i, j: (i, j + nj))],
        out_specs=pl.BlockSpec((tm, tn), lambda i, j: (i, j)),
        compiler_params=_cparams("arbitrary", "arbitrary"),
    )(conv_y, attn, w_conv_out, w_attn_out, gates, gates)


def _outproj_kernel(a_ref, w_ref, x_ref, g_ref, o_ref):
    acc = jnp.dot(a_ref[...], w_ref[...], preferred_element_type=F32)
    o_ref[...] = x_ref[...] + g_ref[0] * acc


def _outproj(merged, w_out, x2d, S, g1):
    T, D = x2d.shape
    K = merged.shape[1]
    tm = _tile(T, (1024, 512, 256, 128))
    tn = _tile(D, (512, 256, 128))
    g, rows, bidx = _mod_operand(g1, S, tm)
    return pl.pallas_call(
        _outproj_kernel, name="outproj",
        out_shape=jax.ShapeDtypeStruct((T, D), F32),
        grid=(T // tm, D // tn),
        in_specs=[pl.BlockSpec((tm, K), lambda i, j: (i, 0)),
                  pl.BlockSpec((K, tn), lambda i, j: (0, j)),
                  pl.BlockSpec((tm, tn), lambda i, j: (i, j)),
                  pl.BlockSpec((1, rows, tn), lambda i, j: (bidx(i), 0, j))],
        out_specs=pl.BlockSpec((tm, tn), lambda i, j: (i, j)),
        compiler_params=_cparams("arbitrary", "arbitrary"),
    )(merged, w_out, x2d, g)


def _pack_bf16_pairs(x):
    half = x.shape[-1] // 2
    lo = pltpu.bitcast(x[:, :half].astype(BF16).astype(F32), I32)
    hi = pltpu.bitcast(x[:, half:].astype(BF16).astype(F32), I32)
    return ((lo >> 16) & jnp.int32(0xFFFF)) | (hi & jnp.int32(-65536))


def _unpack_bf16_pairs(w):
    lo = pltpu.bitcast(w << 16, F32).astype(BF16)
    hi = pltpu.bitcast(w & jnp.int32(-65536), F32).astype(BF16)
    return jnp.concatenate([lo, hi], axis=-1)


def _router_kernel(*refs, NG, NE, nblk):
    npath = len(nblk)
    shared = refs[3 * npath:]
    i = pl.program_id(0)
    lo = 0
    for p in range(npath):
        @pl.when((i >= lo) & (i < lo + nblk[p]))
        def _(path_refs=refs[3 * p:3 * p + 3]):
            _route_rows(*path_refs, *shared, NG=NG, NE=NE)
        lo += nblk[p]


def _route_rows(x_ref, sh_ref, sc_ref, g_ref, wr_ref, br_ref, xm_ref, eid_ref, ew_ref, *, NG, NE):
    xm = _modulated(x_ref[...], g_ref[...], sc_ref[0], sh_ref[0])
    xm_ref[...] = _pack_bf16_pairs(xm)
    logits = jnp.dot(xm, wr_ref[...], preferred_element_type=F32,
                     precision=lax.Precision.HIGHEST) + br_ref[...]
    tm = logits.shape[0]
    epg = NE // NG
    lane = lax.broadcasted_iota(I32, (tm, LANES), 1)
    ninf = -jnp.inf

    def first_argmax(vals):
        mx = jnp.max(vals, axis=1, keepdims=True)
        return mx, jnp.min(jnp.where(vals == mx, lane, LANES), axis=1, keepdims=True)

    gmask = lane < NG
    gmax, g_sel = first_argmax(jnp.where(gmask, logits, ninf))
    g_w = 1.0 / jnp.sum(jnp.where(gmask, jnp.exp(logits - gmax), 0.0), axis=1, keepdims=True)
    emask = (lane >= NG) & (lane < NG + NE) & ((lane - NG) // epg == g_sel)
    el = jnp.where(emask, logits, ninf)
    m1, i1 = first_argmax(el)
    m2, i2 = first_argmax(jnp.where(lane == i1, ninf, el))
    r = jnp.exp(m2 - m1)
    w1 = g_w / (1.0 + r)
    w2 = w1 * r
    eid_ref[...] = jnp.where(lane == 0, i1 - NG, jnp.where(lane == 1, i2 - NG, 0))
    ew_ref[...] = jnp.where(lane == 0, w1, jnp.where(lane == 1, w2, 0.0))


def _router(paths, g, wr, br, NG, NE):
    D = paths[0][0].shape[1]
    tm = next(t for t in (256, 128, 64, 32, 16, 8)
              if all(x.shape[0] % t == 0 and (S % t == 0 or t % S == 0) for x, S, _, _ in paths))
    nblk = tuple(x.shape[0] // tm for x, _, _, _ in paths)
    T = sum(x.shape[0] for x, _, _, _ in paths)
    operands, in_specs, lo = [], [], 0
    for (x, S, shift, scale), nb in zip(paths, nblk):
        sh, rows, bidx = _mod_operand(shift, S, tm)
        sc, _, _ = _mod_operand(scale, S, tm)
        local = lambda i, lo=lo, nb=nb: jnp.clip(i - lo, 0, nb - 1)
        mspec = pl.BlockSpec((1, rows, D), lambda i, local=local, bidx=bidx: (bidx(local(i)), 0, 0))
        operands += [x, sh, sc]
        in_specs += [pl.BlockSpec((tm, D), lambda i, local=local: (local(i), 0)), mspec, mspec]
        lo += nb
    rowspec = lambda n: pl.BlockSpec((tm, n), lambda i: (i, 0))
    return pl.pallas_call(
        functools.partial(_router_kernel, NG=NG, NE=NE, nblk=nblk), name="router",
        out_shape=[jax.ShapeDtypeStruct((T, D // 2), I32), jax.ShapeDtypeStruct((T, LANES), I32),
                   jax.ShapeDtypeStruct((T, LANES), F32)],
        grid=(sum(nblk),),
        in_specs=in_specs + [pl.BlockSpec((1, D), lambda i: (0, 0)),
                             pl.BlockSpec((D, LANES), lambda i: (0, 0)), pl.BlockSpec((1, LANES), lambda i: (0, 0))],
        out_specs=[rowspec(D // 2), rowspec(LANES), rowspec(LANES)],
        compiler_params=_cparams("arbitrary"),
    )(*operands, g.reshape(1, D), wr, br)


def _gather_kernel(tok_ref, nxt_ref, x_hbm, o_ref, buf, sem, *, R):
    i = pl.program_id(0)
    slot = i % 2

    def row_copy(s, r, t):
        return pltpu.make_async_copy(x_hbm.at[pl.ds(t, 1)], buf.at[s, pl.ds(r, 1)], sem.at[s])

    def start_rows(tref, s):
        def body(r, carry):
            row_copy(s, r, tref[0, 0, r]).start()
            return carry
        lax.fori_loop(0, R, body, 0, unroll=8)

    @pl.when((i == 0) & (tok_ref[0, 0, 0] >= 0))
    def _():
        start_rows(tok_ref, 0)

    @pl.when((i + 1 < pl.num_programs(0)) & (nxt_ref[0, 0, 0] >= 0))
    def _():
        start_rows(nxt_ref, 1 - slot)

    @pl.when(tok_ref[0, 0, 0] >= 0)
    def _():
        for r in range(R):
            row_copy(slot, r, 0).wait()
        o_ref[...] = buf[slot]

    @pl.when(tok_ref[0, 0, 0] < 0)
    def _():
        o_ref[...] = jnp.zeros(o_ref.shape, o_ref.dtype)


def _moe_gather(tok_buf, x):
    P = tok_buf.shape[0]
    D = x.shape[1]
    R = _tile(P, (256, 128, 64))
    n = P // R
    toks = tok_buf.reshape(n, 1, R)
    return pl.pallas_call(
        functools.partial(_gather_kernel, R=R), name="moe_gather",
        out_shape=jax.ShapeDtypeStruct((P, D), x.dtype),
        grid=(n,),
        in_specs=[pl.BlockSpec((1, 1, R), lambda i: (i, 0, 0), memory_space=pltpu.SMEM),
                  pl.BlockSpec((1, 1, R), lambda i: (jnp.minimum(i + 1, n - 1), 0, 0), memory_space=pltpu.SMEM),
                  pl.BlockSpec(memory_space=pl.ANY)],
        out_specs=pl.BlockSpec((R, D), lambda i: (i, 0)),
        scratch_shapes=[pltpu.VMEM((2, R, D), x.dtype), pltpu.SemaphoreType.DMA((2,))],
        compiler_params=_cparams("arbitrary"),
    )(toks, toks, x)


def _expert_kernel(be_ref, nu_ref, xs_ref, wg_ref, wu_ref, wd_ref, o_ref, x_sc):
    i = pl.program_id(0)
    j = pl.program_id(1)

    @pl.when(j == 0)
    def _():
        o_ref[...] = jnp.zeros(o_ref.shape, o_ref.dtype)

    @pl.when((j == 0) & (i < nu_ref[0]))
    def _():
        x_sc[...] = _unpack_bf16_pairs(xs_ref[...])

    @pl.when(i < nu_ref[0])
    def _():
        x = x_sc[...]
        gate = jnp.dot(x, wg_ref[0].astype(BF16), preferred_element_type=F32)
        up = jnp.dot(x, wu_ref[0].astype(BF16), preferred_element_type=F32)
        hidden = (gate * jax.nn.sigmoid(gate)) * up
        o_ref[...] += jnp.dot(hidden.astype(BF16), wd_ref[0].astype(BF16), preferred_element_type=F32)


def _moe_experts(block_e, nused, xs, w_gate, w_up, w_down, BM):
    P = xs.shape[0]
    NE, D, DE = w_gate.shape
    tf = _tile(DE, (256, 128))
    nf = DE // tf
    row = lambda i, nu: jnp.minimum(i, nu[0] - 1)
    fj = lambda i, j, nu: jnp.where(i < nu[0], j, nf - 1)
    grid_spec = pltpu.PrefetchScalarGridSpec(
        num_scalar_prefetch=2,
        grid=(P // BM, nf),
        in_specs=[pl.BlockSpec((BM, D // 2), lambda i, j, be, nu: (row(i, nu), 0)),
                  pl.BlockSpec((1, D, tf), lambda i, j, be, nu: (be[row(i, nu)], 0, fj(i, j, nu))),
                  pl.BlockSpec((1, D, tf), lambda i, j, be, nu: (be[row(i, nu)], 0, fj(i, j, nu))),
                  pl.BlockSpec((1, tf, D), lambda i, j, be, nu: (be[row(i, nu)], fj(i, j, nu), 0))],
        out_specs=pl.BlockSpec((BM, D), lambda i, j, be, nu: (i, 0)),
        scratch_shapes=[pltpu.VMEM((BM, D), BF16)],
    )
    return pl.pallas_call(
        _expert_kernel, name="moe_experts", grid_spec=grid_spec,
        out_shape=jax.ShapeDtypeStruct((P, D), F32),
        compiler_params=_cparams("arbitrary", "arbitrary"),
    )(block_e, nused, xs, w_gate, w_up, w_down)


def _combine_kernel(pos_ref, nxt_ref, x_ref, g_ref, ew_ref, ys_hbm, o_ref, buf, sem, *, tc):
    i = pl.program_id(0)
    slot = i % 2

    def row_copy(s, r, jj, p):
        return pltpu.make_async_copy(ys_hbm.at[pl.ds(p, 1)], buf.at[s, jj, pl.ds(r, 1)], sem.at[s])

    def start_rows(pref, s):
        def body(r, carry):
            for jj in range(TOPK_EXPERTS):
                row_copy(s, r, jj, pref[0, 0, TOPK_EXPERTS * r + jj]).start()
            return carry
        lax.fori_loop(0, tc, body, 0, unroll=4)

    @pl.when(i == 0)
    def _():
        start_rows(pos_ref, 0)

    @pl.when(i + 1 < pl.num_programs(0))
    def _():
        start_rows(nxt_ref, 1 - slot)

    for r in range(tc):
        for jj in range(TOPK_EXPERTS):
            row_copy(slot, r, jj, 0).wait()
    moe = ew_ref[:, 0:1] * buf[slot, 0] + ew_ref[:, 1:2] * buf[slot, 1]
    o_ref[...] = x_ref[...] + g_ref[0] * moe


def _moe_combine(pos, x2d, S, g2, ew_all, tok_off, ys):
    T, D = x2d.shape
    tc = _tile(T, (256, 128, 64))
    assert tok_off % tc == 0
    n = T // tc
    g, rows, bidx = _mod_operand(g2, S, tc)
    posb = pos.reshape(n, 1, TOPK_EXPERTS * tc)
    pspec = lambda nxt: pl.BlockSpec((1, 1, TOPK_EXPERTS * tc), lambda i: (jnp.minimum(i + nxt, n - 1), 0, 0),
                                     memory_space=pltpu.SMEM)
    return pl.pallas_call(
        functools.partial(_combine_kernel, tc=tc), name="moe_combine",
        out_shape=jax.ShapeDtypeStruct((T, D), F32),
        grid=(n,),
        in_specs=[pspec(0), pspec(1),
                  pl.BlockSpec((tc, D), lambda i: (i, 0)),
                  pl.BlockSpec((1, rows, D), lambda i: (bidx(i), 0, 0)),
                  pl.BlockSpec((tc, LANES), lambda i: (i + tok_off // tc, 0)),
                  pl.BlockSpec(memory_space=pl.ANY)],
        out_specs=pl.BlockSpec((tc, D), lambda i: (i, 0)),
        scratch_shapes=[pltpu.VMEM((2, TOPK_EXPERTS, tc, D), F32), pltpu.SemaphoreType.DMA((2,))],
        compiler_params=_cparams("arbitrary"),
    )(posb, posb, x2d, g, ew_all, ys)


def _route_meta(eid, NE, BM):
    n_slots = eid.shape[0] * TOPK_EXPERTS
    slot_e = eid.reshape(-1)
    slot_id = jnp.arange(n_slots, dtype=I32)
    se, order = lax.sort((slot_e, slot_id), num_keys=1, is_stable=True)
    experts = jnp.arange(NE, dtype=I32)
    onehot = se[:, None] == experts[None, :]
    counts = jnp.sum(onehot, axis=0, dtype=I32)
    start = jnp.cumsum(counts) - counts
    padded = (counts + BM - 1) // BM * BM
    pend = jnp.cumsum(padded)
    pstart = pend - padded
    P = -(-n_slots // BM) * BM + NE * BM
    nblk = P // BM
    blk_row = jnp.arange(nblk, dtype=I32) * BM
    block_e = jnp.minimum(jnp.sum(pend[None, :] <= blk_row[:, None], axis=1, dtype=I32), NE - 1)
    nused = (pend[-1] // BM).astype(I32).reshape(1)
    dest_sorted = slot_id + jnp.sum(jnp.where(onehot, (pstart - start)[None, :], 0), axis=1, dtype=I32)
    _, pos = lax.sort((order, dest_sorted), num_keys=1)
    src0 = start[block_e] + blk_row - pstart[block_e]
    order_pad = jnp.concatenate([order, jnp.zeros((BM,), I32)])
    runs = jax.vmap(lambda s: lax.dynamic_slice(order_pad, (s,), (BM,)))(jnp.clip(src0, 0, n_slots))
    src = src0[:, None] + jnp.arange(BM, dtype=I32)[None, :]
    valid = src < (start + counts)[block_e][:, None]
    tok_buf = jnp.where(valid, runs // TOPK_EXPERTS, 0)
    tok_buf = jnp.where((blk_row < pend[-1])[:, None], tok_buf, -1).reshape(P).astype(I32)
    return tok_buf, pos, block_e, nused


def _layer_front(x, mod, conv_prev, past, W, dims):
    B, S, D = x.shape
    T = B * S
    NH, KVH, hd, NIH, DI, NG, NE = dims
    sh1, sc1, g1, sh2, sc2, g2 = jnp.split(mod, 6, axis=-1)
    x2d = x.reshape(T, D)
    xm = _modulate(x2d, S, W["norm1_g"], sh1, sc1)
    hbc = _matmul(xm, W["w_in_conv"], BF16, name="inproj_conv")
    raw = _matmul(xm, W["w_in_attn"], F32, name="inproj_attn")
    gates = _matmul(xm, W["w_in_gate"], BF16, act="sigmoid", name="inproj_gate")
    q, k, v, ik, kb, vb, ikb, iq, iw = _attn_post(raw, W["q_norm_g"], W["k_norm_g"], W["ik_norm_g"],
                                                  NH, KVH, hd, NIH, DI)
    C = W["conv_w"].shape[1]
    conv_y, conv_state = _short_conv(hbc.reshape(B, S, 3 * C), conv_prev, W["conv_w"])
    r3 = lambda a: a.reshape(B, S, a.shape[-1])
    if past is None:
        k_all, v_all, ik_all = r3(kb), r3(vb), r3(ikb)
        L_true, causal = S, True
    else:
        k_past, v_past, ik_past = past
        Pn = k_past.shape[1]
        L_true, causal = Pn + S, False
        cat = lambda old, new: jnp.concatenate([old.reshape(B, Pn, -1).astype(BF16), r3(new)], axis=1)
        k_all, v_all, ik_all = cat(k_past, kb), cat(v_past, vb), cat(ik_past, ikb)
    attn = _attention(r3(q), r3(iq), r3(iw), k_all, v_all, ik_all, NH=NH, KVH=KVH, hd=hd, NIH=NIH, DI=DI,
                      topk=min(TOPK_MAX, L_true // 4), causal=causal, L_true=L_true)
    merged = _merge(conv_y.reshape(T, C), attn.reshape(T, NH * hd), W["w_conv_out"], W["w_attn_out"], gates)
    x_mid = _outproj(merged, W["w_out"], x2d, S, g1)
    outs = (k.reshape(B, S, KVH, hd), v.reshape(B, S, KVH, hd), ik.reshape(B, S, DI), conv_state)
    return x_mid, (sh2, sc2, g2), outs


def kernel(x_prompt, x_sample, cache_k, cache_v, cache_idx_k, state_conv, c_prompt, c_sample, norm1_g, norm2_g, w_ada, b_ada, w_in, conv_w, q_norm_g, k_norm_g, ik_norm_g, w_conv_out, w_attn_out, w_out, w_router_group, b_router_group, w_router_expert, b_router_expert, w_gate, w_up, w_down):
    Bp, Sp, D = x_prompt.shape
    Bs, Ss, _ = x_sample.shape
    CW, C = conv_w.shape
    hd = q_norm_g.shape[0]
    DI = ik_norm_g.shape[0]
    KVH = cache_k.shape[2]
    NH = w_attn_out.shape[0] // hd
    NG = b_router_group.shape[0]
    NE = b_router_expert.shape[0]
    off_q = 3 * C
    off_iq = off_q + NH * hd + 2 * KVH * hd
    off_gate = w_in.shape[1] - 2 * D
    NIH = (off_gate - off_iq - DI) // (DI + 1)
    dims = (NH, KVH, hd, NIH, DI, NG, NE)

    na = off_gate - off_q
    na_pad = -(-na // LANES) * LANES
    rpad = LANES - NG - NE
    W = {
        "norm1_g": norm1_g, "norm2_g": norm2_g, "conv_w": conv_w,
        "q_norm_g": q_norm_g, "k_norm_g": k_norm_g, "ik_norm_g": ik_norm_g,
        "w_in_conv": w_in[:, :off_q].astype(BF16),
        "w_in_attn": jnp.pad(w_in[:, off_q:off_gate], ((0, 0), (0, na_pad - na))).astype(BF16),
        "w_in_gate": w_in[:, off_gate:].astype(BF16),
        "w_conv_out": w_conv_out.astype(BF16), "w_attn_out": w_attn_out.astype(BF16),
        "w_out": w_out.astype(BF16),
        "w_router": jnp.pad(jnp.concatenate([w_router_group, w_router_expert], axis=1), ((0, 0), (0, rpad))),
        "b_router": jnp.pad(jnp.concatenate([b_router_group, b_router_expert]), (0, rpad)).reshape(1, LANES),
    }

    nb = Bp + Bs
    c_all = jnp.pad(jnp.concatenate([c_prompt, c_sample], axis=0), ((0, -nb % 8), (0, 0)))
    mod = _ada(c_all, w_ada, b_ada)

    conv_zero = jnp.zeros((Bp, CW - 1, C), F32)
    xmid_p, (sh2_p, sc2_p, g2_p), outs_p = _layer_front(x_prompt, mod[:Bp], conv_zero, None, W, dims)
    xmid_s, (sh2_s, sc2_s, g2_s), outs_s = _layer_front(
        x_sample, mod[Bp:nb], state_conv, (cache_k, cache_v, cache_idx_k), W, dims)

    Tp = Bp * Sp
    xm2, eid, ew = _router([(xmid_p, Sp, sh2_p, sc2_p), (xmid_s, Ss, sh2_s, sc2_s)],
                           norm2_g, W["w_router"], W["b_router"], NG, NE)
    tok_buf, pos, block_e, nused = _route_meta(eid[:, :TOPK_EXPERTS], NE, MOE_BM)
    xs = _moe_gather(tok_buf, xm2)
    ys = _moe_experts(block_e, nused, xs, w_gate, w_up, w_down, MOE_BM)
    y_p = _moe_combine(pos[:TOPK_EXPERTS * Tp], xmid_p, Sp, g2_p, ew, 0, ys).reshape(Bp, Sp, D)
    y_s = _moe_combine(pos[TOPK_EXPERTS * Tp:], xmid_s, Ss, g2_s, ew, Tp, ys).reshape(Bs, Ss, D)

    k_p, v_p, ik_p, conv_p = outs_p
    k_s, v_s, ik_s, conv_s = outs_s
    return (y_p, y_s, k_p, v_p, ik_p, conv_p, k_s, v_s, ik_s, conv_s)
```

```python
import functools

import jax
import jax.numpy as jnp
from jax import lax
from jax.experimental import pallas as pl
from jax.experimental.pallas import tpu as pltpu

F32 = jnp.float32
BF16 = jnp.bfloat16
I32 = jnp.int32

CHUNK = 64
EPS = 1e-6
TOPK_MAX = 256
TOPK_EXPERTS = 2
LANES = 128
VMEM_LIMIT = 56 * 1024 * 1024
INT_MIN = -2 ** 31
NEG = -1e30
LOG2E = 1.4426950408889634
MOE_BM = 512
ATTN_KEY_TILE = 256
IDX_HEADS_PER_DOT = 16
NT_DIMS = (((1,), (1,)), ((), ()))


def _tile(n, prefs):
    for t in prefs:
        if n % t == 0:
            return t
    return n


def _cparams(*sem):
    return pltpu.CompilerParams(dimension_semantics=sem, vmem_limit_bytes=VMEM_LIMIT)


def _mod_operand(m, S, tm):
    B, D = m.shape
    if S % tm == 0:
        return m.reshape(B, 1, D), 1, (lambda i: (i * tm) // S)
    assert tm % S == 0
    full = jnp.broadcast_to(m[:, None, :], (B, S, D)).reshape((B * S) // tm, tm, D)
    return full, tm, (lambda i: i)


def _ada_kernel(c_ref, w_ref, b_ref, o_ref):
    o_ref[...] = jnp.dot(c_ref[...], w_ref[...], preferred_element_type=F32,
                         precision=lax.Precision.HIGHEST) + b_ref[...]


def _ada(c, w, b):
    M, D = c.shape
    N = w.shape[1]
    tn = _tile(N, (512, 256, 128))
    return pl.pallas_call(
        _ada_kernel, name="ada",
        out_shape=jax.ShapeDtypeStruct((M, N), F32),
        grid=(N // tn,),
        in_specs=[pl.BlockSpec((M, D), lambda j: (0, 0)),
                  pl.BlockSpec((D, tn), lambda j: (0, j)),
                  pl.BlockSpec((1, tn), lambda j: (0, j))],
        out_specs=pl.BlockSpec((M, tn), lambda j: (0, j)),
        compiler_params=_cparams("arbitrary"),
    )(c, w, b.reshape(1, N))


def _modulated(x, g, scale, shift):
    ms = jnp.mean(x * x, axis=-1, keepdims=True)
    return (x * lax.rsqrt(ms + EPS)) * g * (1.0 + scale) + shift


def _modulate_kernel(x_ref, g_ref, sh_ref, sc_ref, o_ref):
    o_ref[...] = _modulated(x_ref[...], g_ref[...], sc_ref[0], sh_ref[0]).astype(o_ref.dtype)


def _modulate(x2d, S, g, shift, scale):
    T, D = x2d.shape
    tm = _tile(S, (512, 256, 128, 64))
    sh, rows, bidx = _mod_operand(shift, S, tm)
    sc, _, _ = _mod_operand(scale, S, tm)
    mspec = pl.BlockSpec((1, rows, D), lambda i: (bidx(i), 0, 0))
    return pl.pallas_call(
        _modulate_kernel, name="modulate",
        out_shape=jax.ShapeDtypeStruct((T, D), BF16),
        grid=(T // tm,),
        in_specs=[pl.BlockSpec((tm, D), lambda i: (i, 0)),
                  pl.BlockSpec((1, D), lambda i: (0, 0)), mspec, mspec],
        out_specs=pl.BlockSpec((tm, D), lambda i: (i, 0)),
        compiler_params=_cparams("arbitrary"),
    )(x2d, g.reshape(1, D), sh, sc)


def _mm_kernel(a_ref, b_ref, o_ref, *, act):
    acc = jnp.dot(a_ref[...], b_ref[...], preferred_element_type=F32)
    if act == "sigmoid":
        acc = jax.nn.sigmoid(acc)
    o_ref[...] = acc.astype(o_ref.dtype)


def _matmul(a, b, out_dtype, act="none", name="matmul"):
    M, K = a.shape
    N = b.shape[1]
    tm = _tile(M, (1024, 512, 256, 128))
    tn = _tile(N, (512, 384, 256, 128))
    return pl.pallas_call(
        functools.partial(_mm_kernel, act=act), name=name,
        out_shape=jax.ShapeDtypeStruct((M, N), out_dtype),
        grid=(M // tm, N // tn),
        in_specs=[pl.BlockSpec((tm, K), lambda i, j: (i, 0)),
                  pl.BlockSpec((K, tn), lambda i, j: (0, j))],
        out_specs=pl.BlockSpec((tm, tn), lambda i, j: (i, j)),
        compiler_params=_cparams("arbitrary", "arbitrary"),
    )(a, b)


def _attn_post_kernel(raw_ref, qg_ref, kg_ref, ikg_ref,
                      q_ref, k_ref, v_ref, ik_ref, kb_ref, vb_ref, ikb_ref, iq_ref, iw_ref,
                      *, NH, KVH, hd, NIH, DI):
    def headnorm(col, g):
        seg = raw_ref[:, col:col + hd]
        ms = jnp.mean(seg * seg, axis=-1, keepdims=True)
        return seg * lax.rsqrt(ms + EPS) * g

    qg = qg_ref[...] * (LOG2E * hd ** -0.5)
    for h in range(NH):
        q_ref[:, h * hd:(h + 1) * hd] = (headnorm(h * hd, 1.0) * qg).astype(q_ref.dtype)
    off_k = NH * hd
    off_v = off_k + KVH * hd
    for h in range(KVH):
        kn = headnorm(off_k + h * hd, kg_ref[...])
        k_ref[:, h * hd:(h + 1) * hd] = kn
        kb_ref[:, h * hd:(h + 1) * hd] = kn.astype(kb_ref.dtype)
    v = raw_ref[:, off_v:off_v + KVH * hd]
    v_ref[...] = v
    vb_ref[...] = v.astype(vb_ref.dtype)
    off_iq = off_v + KVH * hd
    off_ik = off_iq + NIH * DI
    off_iw = off_ik + DI
    iq_ref[...] = (raw_ref[:, off_iq:off_ik] * (DI ** -0.5)).astype(iq_ref.dtype)
    ikr = raw_ref[:, off_ik:off_iw]
    ikn = ikr * lax.rsqrt(jnp.mean(ikr * ikr, axis=-1, keepdims=True) + EPS) * ikg_ref[...]
    ik_ref[...] = ikn
    ikb_ref[...] = ikn.astype(ikb_ref.dtype)
    iw_ref[...] = raw_ref[:, off_iw:off_iw + NIH] * (NIH ** -0.5)


def _attn_post(raw, q_norm_g, k_norm_g, ik_norm_g, NH, KVH, hd, NIH, DI):
    T, NA = raw.shape
    tr = _tile(T, (256, 128, 64))
    outs = [(NH * hd, BF16), (KVH * hd, F32), (KVH * hd, F32), (DI, F32),
            (KVH * hd, BF16), (KVH * hd, BF16), (DI, BF16), (NIH * DI, BF16), (NIH, F32)]
    return pl.pallas_call(
        functools.partial(_attn_post_kernel, NH=NH, KVH=KVH, hd=hd, NIH=NIH, DI=DI), name="attn_post",
        out_shape=[jax.ShapeDtypeStruct((T, n), dt) for n, dt in outs],
        grid=(T // tr,),
        in_specs=[pl.BlockSpec((tr, NA), lambda i: (i, 0)),
                  pl.BlockSpec((1, hd), lambda i: (0, 0)),
                  pl.BlockSpec((1, hd), lambda i: (0, 0)),
                  pl.BlockSpec((1, DI), lambda i: (0, 0))],
        out_specs=[pl.BlockSpec((tr, n), lambda i: (i, 0)) for n, _ in outs],
        compiler_params=_cparams("arbitrary"),
    )(raw, q_norm_g.reshape(1, hd), k_norm_g.reshape(1, hd), ik_norm_g.reshape(1, DI))


def _conv_kernel(h_ref, b_ref, c_ref, w_ref, prev_ref, y_ref, st_ref, ubuf, *, ts, CW):
    lo = 8 - (CW - 1)

    @pl.when(pl.program_id(1) == 0)
    def _():
        ubuf[lo:8, :] = prev_ref[0]

    @pl.when(pl.program_id(1) > 0)
    def _():
        ubuf[lo:8, :] = ubuf[ts + lo:ts + 8, :]

    ubuf[8:8 + ts, :] = c_ref[0].astype(F32) * h_ref[0].astype(F32)
    y = w_ref[0:1, :] * ubuf[lo:lo + ts, :]
    for j in range(1, CW):
        y = y + w_ref[j:j + 1, :] * ubuf[lo + j:lo + j + ts, :]
    y_ref[0] = (b_ref[0].astype(F32) * y).astype(y_ref.dtype)
    st_ref[0] = ubuf[ts + lo:ts + 8, :]


def _short_conv(hbc, conv_prev, conv_w):
    B, S, C3 = hbc.shape
    CW, C = conv_w.shape
    ts = _tile(S, (512, 256, 128, 64))
    col = lambda k: pl.BlockSpec((1, ts, C), lambda b, s: (b, s, k))
    return pl.pallas_call(
        functools.partial(_conv_kernel, ts=ts, CW=CW), name="short_conv",
        out_shape=[jax.ShapeDtypeStruct((B, S, C), BF16), jax.ShapeDtypeStruct((B, CW - 1, C), F32)],
        grid=(B, S // ts),
        in_specs=[col(0), col(1), col(2),
                  pl.BlockSpec((CW, C), lambda b, s: (0, 0)),
                  pl.BlockSpec((1, CW - 1, C), lambda b, s: (b, 0, 0))],
        out_specs=[pl.BlockSpec((1, ts, C), lambda b, s: (b, s, 0)),
                   pl.BlockSpec((1, CW - 1, C), lambda b, s: (b, 0, 0))],
        scratch_shapes=[pltpu.VMEM((ts + 8, C), F32)],
        compiler_params=_cparams("arbitrary", "arbitrary"),
    )(hbc, hbc, hbc, conv_w, conv_prev)


def _attn_kernel(q_ref, iq_ref, iwt_ref, k_ref, vt_ref, ik_ref, o_ref, keys_ref, bias_sc, m_sc, l_sc, acc_sc,
                 *, QB, TL, NH, KVH, hd, NIH, DI, topk, causal, L_true, Lp):
    i = pl.program_id(1)
    rep = NH // KVH
    if causal:
        qrow = i * QB + lax.broadcasted_iota(I32, (1, QB), 1)
        vis = (qrow // CHUNK + 1) * CHUNK
        nkt = ((i + 1) * QB + TL - 1) // TL
    else:
        vis = jnp.full((1, QB), L_true, I32)
        nkt = Lp // TL

    def score_tile(t, carry):
        off = pl.multiple_of(t * TL, TL)
        ikt = ik_ref[0, pl.ds(off, TL), :]
        acc = jnp.zeros((TL, QB), F32)
        for h0 in range(0, NIH, IDX_HEADS_PER_DOT):
            nh = min(IDX_HEADS_PER_DOT, NIH - h0)
            d = lax.dot_general(ikt, iq_ref[0, 0, h0 * QB:(h0 + nh) * QB, :], NT_DIMS, preferred_element_type=F32)
            for h in range(h0, h0 + nh):
                acc = acc + iwt_ref[0, h:h + 1, :] * jnp.maximum(d[:, (h - h0) * QB:(h - h0 + 1) * QB], 0.0)
        bits = pltpu.bitcast(acc, I32)
        key = bits ^ ((bits >> 31) & jnp.int32(0x7FFFFFFF))
        kid = off + lax.broadcasted_iota(I32, (TL, 1), 0)
        keys_ref[pl.ds(off, TL), :] = jnp.where(kid < vis, key, jnp.int32(INT_MIN))
        return carry

    lax.fori_loop(0, nkt, score_tile, 0)

    def count_ge(cand):
        def body(t, c):
            off = pl.multiple_of(t * TL, TL)
            hit = jnp.where(keys_ref[pl.ds(off, TL), :] >= cand, 1, 0).astype(I32)
            return c + jnp.sum(hit.reshape(TL // 8, 8, QB), axis=0)
        c = lax.fori_loop(0, nkt, body, jnp.zeros((8, QB), I32))
        return jnp.sum(c, axis=0, keepdims=True)

    def bit_step(bi, thr):
        cand = thr + (jnp.int32(1) << (31 - bi))
        return jnp.where(count_ge(cand) >= topk, cand, thr)

    thr = lax.fori_loop(0, 32, bit_step, jnp.full((1, QB), INT_MIN, I32))
    thr = jnp.maximum(thr, jnp.int32(INT_MIN + 1))

    def tie_search(_):
        need = topk - count_ge(thr + 1)

        def count_eq_below(cand):
            def body(t, c):
                off = pl.multiple_of(t * TL, TL)
                kid = off + lax.broadcasted_iota(I32, (TL, 1), 0)
                hit = jnp.where((keys_ref[pl.ds(off, TL), :] == thr) & (kid < cand), 1, 0).astype(I32)
                return c + jnp.sum(hit.reshape(TL // 8, 8, QB), axis=0)
            c = lax.fori_loop(0, nkt, body, jnp.zeros((8, QB), I32))
            return jnp.sum(c, axis=0, keepdims=True)

        def idx_step(bi, lo):
            cand = lo + (jnp.int32(1) << (IDX_BITS - 1 - bi))
            return jnp.where(count_eq_below(cand) < need, cand, lo)

        return lax.fori_loop(0, IDX_BITS, idx_step, jnp.zeros((1, QB), I32))

    IDX_BITS = max(1, Lp.bit_length())
    tie_idx = lax.cond(jnp.max(count_ge(thr)) > topk, tie_search,
                       lambda _: jnp.full((1, QB), Lp, I32), 0)

    m_sc[...] = jnp.full(m_sc.shape, NEG, F32)
    l_sc[...] = jnp.zeros(l_sc.shape, F32)
    acc_sc[...] = jnp.zeros(acc_sc.shape, F32)

    def kv_tile(t, carry):
        off = pl.multiple_of(t * TL, TL)
        kid = off + lax.broadcasted_iota(I32, (TL, 1), 0)
        kt_keys = keys_ref[pl.ds(off, TL), :]
        keep = (kt_keys > thr) | ((kt_keys == thr) & (kid <= tie_idx))
        bias_sc[...] = jnp.where(keep, 0.0, NEG)
        for g in range(KVH):
            kt = k_ref[0, pl.ds(off, TL), g * hd:(g + 1) * hd]
            vt = vt_ref[0, g * hd:(g + 1) * hd, pl.ds(off, TL)]
            for r in range(rep):
                h = g * rep + r
                s = lax.dot_general(kt, q_ref[0, :, h * hd:(h + 1) * hd], NT_DIMS,
                                    preferred_element_type=F32) + bias_sc[...]
                m_old = m_sc[h:h + 1, :]
                m_new = jnp.maximum(m_old, jnp.max(s, axis=0, keepdims=True))
                alpha = jnp.exp2(m_old - m_new)
                p = jnp.exp2(s - m_new)
                l_sc[h:h + 1, :] = alpha * l_sc[h:h + 1, :] + jnp.sum(p, axis=0, keepdims=True)
                acc_sc[h] = alpha * acc_sc[h] + jnp.dot(vt, p.astype(vt.dtype), preferred_element_type=F32)
                m_sc[h:h + 1, :] = m_new
        return carry

    lax.fori_loop(0, nkt, kv_tile, 0)
    for h in range(NH):
        o = acc_sc[h] / l_sc[h:h + 1, :]
        o_ref[0, :, h * hd:(h + 1) * hd] = o.T.astype(o_ref.dtype)


def _attention(q, iq, iw, k_all, v_all, ik_all, *, NH, KVH, hd, NIH, DI, topk, causal, L_true):
    B, S0, _ = q.shape
    QB = LANES
    S = -(-S0 // QB) * QB
    TL = ATTN_KEY_TILE
    Lp = -(-k_all.shape[1] // TL) * TL
    pad_s = lambda a: jnp.pad(a, ((0, 0), (0, S - S0), (0, 0)))
    pad_l = lambda a: jnp.pad(a, ((0, 0), (0, Lp - a.shape[1]), (0, 0)))
    q, iq, iw = pad_s(q), pad_s(iq), pad_s(iw)
    k_all, v_all, ik_all = pad_l(k_all), pad_l(v_all), pad_l(ik_all)
    iq_heads = iq.reshape(B, S // QB, QB, NIH, DI).transpose(0, 1, 3, 2, 4).reshape(B, S // QB, NIH * QB, DI)
    qspec = lambda n: pl.BlockSpec((1, QB, n), lambda b, i: (b, i, 0))
    kspec = lambda n: pl.BlockSpec((1, Lp, n), lambda b, i: (b, 0, 0))
    out = pl.pallas_call(
        functools.partial(_attn_kernel, QB=QB, TL=TL, NH=NH, KVH=KVH, hd=hd, NIH=NIH, DI=DI,
                          topk=topk, causal=causal, L_true=L_true, Lp=Lp), name="sparse_attn",
        out_shape=jax.ShapeDtypeStruct((B, S, NH * hd), BF16),
        grid=(B, S // QB),
        in_specs=[qspec(NH * hd),
                  pl.BlockSpec((1, 1, NIH * QB, DI), lambda b, i: (b, i, 0, 0)),
                  pl.BlockSpec((1, NIH, QB), lambda b, i: (b, 0, i)),
                  kspec(KVH * hd),
                  pl.BlockSpec((1, KVH * hd, Lp), lambda b, i: (b, 0, 0)),
                  kspec(DI)],
        out_specs=qspec(NH * hd),
        scratch_shapes=[pltpu.VMEM((Lp, QB), I32), pltpu.VMEM((TL, QB), F32), pltpu.VMEM((NH, QB), F32),
                        pltpu.VMEM((NH, QB), F32), pltpu.VMEM((NH, hd, QB), F32)],
        compiler_params=_cparams("arbitrary", "arbitrary"),
    )(q, iq_heads, jnp.swapaxes(iw, 1, 2), k_all, jnp.swapaxes(v_all, 1, 2), ik_all)
    return out[:, :S0]


def _merge_kernel(cy_ref, at_ref, wc_ref, wa_ref, gc_ref, ga_ref, o_ref):
    a = jnp.dot(cy_ref[...], wc_ref[...], preferred_element_type=F32)
    b = jnp.dot(at_ref[...], wa_ref[...], preferred_element_type=F32)
    o_ref[...] = (gc_ref[...].astype(F32) * a + ga_ref[...].astype(F32) * b).astype(o_ref.dtype)


def _merge(conv_y, attn, w_conv_out, w_attn_out, gates):
    T, C = conv_y.shape
    A = attn.shape[1]
    D = w_conv_out.shape[1]
    tm = _tile(T, (1024, 512, 256, 128))
    tn = _tile(D, (512, 256, 128))
    nj = D // tn
    return pl.pallas_call(
        _merge_kernel, name="merge",
        out_shape=jax.ShapeDtypeStruct((T, D), BF16),
        grid=(T // tm, nj),
        in_specs=[pl.BlockSpec((tm, C), lambda i, j: (i, 0)),
                  pl.BlockSpec((tm, A), lambda i, j: (i, 0)),
                  pl.BlockSpec((C, tn), lambda i, j: (0, j)),
                  pl.BlockSpec((A, tn), lambda i, j: (0, j)),
                  pl.BlockSpec((tm, tn), lambda i, j: (i, j)),
                  pl.BlockSpec((tm, tn), lambda i, j: (i, j + nj))],
        out_specs=pl.BlockSpec((tm, tn), lambda i, j: (i, j)),
        compiler_params=_cparams("arbitrary", "arbitrary"),
    )(conv_y, attn, w_conv_out, w_attn_out, gates, gates)


def _outproj_kernel(a_ref, w_ref, x_ref, g_ref, o_ref):
    acc = jnp.dot(a_ref[...], w_ref[...], preferred_element_type=F32)
    o_ref[...] = x_ref[...] + g_ref[0] * acc


def _outproj(merged, w_out, x2d, S, g1):
    T, D = x2d.shape
    K = merged.shape[1]
    tm = _tile(T, (1024, 512, 256, 128))
    tn = _tile(D, (512, 256, 128))
    g, rows, bidx = _mod_operand(g1, S, tm)
    return pl.pallas_call(
        _outproj_kernel, name="outproj",
        out_shape=jax.ShapeDtypeStruct((T, D), F32),
        grid=(T // tm, D // tn),
        in_specs=[pl.BlockSpec((tm, K), lambda i, j: (i, 0)),
                  pl.BlockSpec((K, tn), lambda i, j: (0, j)),
                  pl.BlockSpec((tm, tn), lambda i, j: (i, j)),
                  pl.BlockSpec((1, rows, tn), lambda i, j: (bidx(i), 0, j))],
        out_specs=pl.BlockSpec((tm, tn), lambda i, j: (i, j)),
        compiler_params=_cparams("arbitrary", "arbitrary"),
    )(merged, w_out, x2d, g)


def _pack_bf16_pairs(x):
    half = x.shape[-1] // 2
    lo = pltpu.bitcast(x[:, :half].astype(BF16).astype(F32), I32)
    hi = pltpu.bitcast(x[:, half:].astype(BF16).astype(F32), I32)
    return ((lo >> 16) & jnp.int32(0xFFFF)) | (hi & jnp.int32(-65536))


def _unpack_bf16_pairs(w):
    lo = pltpu.bitcast(w << 16, F32).astype(BF16)
    hi = pltpu.bitcast(w & jnp.int32(-65536), F32).astype(BF16)
    return jnp.concatenate([lo, hi], axis=-1)


def _router_kernel(*refs, NG, NE, nblk):
    npath = len(nblk)
    shared = refs[3 * npath:]
    i = pl.program_id(0)
    lo = 0
    for p in range(npath):
        @pl.when((i >= lo) & (i < lo + nblk[p]))
        def _(path_refs=refs[3 * p:3 * p + 3]):
            _route_rows(*path_refs, *shared, NG=NG, NE=NE)
        lo += nblk[p]


def _route_rows(x_ref, sh_ref, sc_ref, g_ref, wr_ref, br_ref, xm_ref, eid_ref, ew_ref, *, NG, NE):
    xm = _modulated(x_ref[...], g_ref[...], sc_ref[0], sh_ref[0])
    xm_ref[...] = _pack_bf16_pairs(xm)
    logits = jnp.dot(xm, wr_ref[...], preferred_element_type=F32,
                     precision=lax.Precision.HIGHEST) + br_ref[...]
    tm = logits.shape[0]
    epg = NE // NG
    lane = lax.broadcasted_iota(I32, (tm, LANES), 1)
    ninf = -jnp.inf

    def first_argmax(vals):
        mx = jnp.max(vals, axis=1, keepdims=True)
        return mx, jnp.min(jnp.where(vals == mx, lane, LANES), axis=1, keepdims=True)

    gmask = lane < NG
    gmax, g_sel = first_argmax(jnp.where(gmask, logits, ninf))
    g_w = 1.0 / jnp.sum(jnp.where(gmask, jnp.exp(logits - gmax), 0.0), axis=1, keepdims=True)
    emask = (lane >= NG) & (lane < NG + NE) & ((lane - NG) // epg == g_sel)
    el = jnp.where(emask, logits, ninf)
    m1, i1 = first_argmax(el)
    m2, i2 = first_argmax(jnp.where(lane == i1, ninf, el))
    r = jnp.exp(m2 - m1)
    w1 = g_w / (1.0 + r)
    w2 = w1 * r
    eid_ref[...] = jnp.where(lane == 0, i1 - NG, jnp.where(lane == 1, i2 - NG, 0))
    ew_ref[...] = jnp.where(lane == 0, w1, jnp.where(lane == 1, w2, 0.0))


def _router(paths, g, wr, br, NG, NE):
    D = paths[0][0].shape[1]
    tm = next(t for t in (256, 128, 64, 32, 16, 8)
              if all(x.shape[0] % t == 0 and (S % t == 0 or t % S == 0) for x, S, _, _ in paths))
    nblk = tuple(x.shape[0] // tm for x, _, _, _ in paths)
    T = sum(x.shape[0] for x, _, _, _ in paths)
    operands, in_specs, lo = [], [], 0
    for (x, S, shift, scale), nb in zip(paths, nblk):
        sh, rows, bidx = _mod_operand(shift, S, tm)
        sc, _, _ = _mod_operand(scale, S, tm)
        local = lambda i, lo=lo, nb=nb: jnp.clip(i - lo, 0, nb - 1)
        mspec = pl.BlockSpec((1, rows, D), lambda i, local=local, bidx=bidx: (bidx(local(i)), 0, 0))
        operands += [x, sh, sc]
        in_specs += [pl.BlockSpec((tm, D), lambda i, local=local: (local(i), 0)), mspec, mspec]
        lo += nb
    rowspec = lambda n: pl.BlockSpec((tm, n), lambda i: (i, 0))
    return pl.pallas_call(
        functools.partial(_router_kernel, NG=NG, NE=NE, nblk=nblk), name="router",
        out_shape=[jax.ShapeDtypeStruct((T, D // 2), I32), jax.ShapeDtypeStruct((T, LANES), I32),
                   jax.ShapeDtypeStruct((T, LANES), F32)],
        grid=(sum(nblk),),
        in_specs=in_specs + [pl.BlockSpec((1, D), lambda i: (0, 0)),
                             pl.BlockSpec((D, LANES), lambda i: (0, 0)), pl.BlockSpec((1, LANES), lambda i: (0, 0))],
        out_specs=[rowspec(D // 2), rowspec(LANES), rowspec(LANES)],
        compiler_params=_cparams("arbitrary"),
    )(*operands, g.reshape(1, D), wr, br)


def _gather_kernel(tok_ref, nxt_ref, x_hbm, o_ref, buf, sem, *, R):
    i = pl.program_id(0)
    slot = i % 2

    def row_copy(s, r, t):
        return pltpu.make_async_copy(x_hbm.at[pl.ds(t, 1)], buf.at[s, pl.ds(r, 1)], sem.at[s])

    def start_rows(tref, s):
        def body(r, carry):
            row_copy(s, r, tref[0, 0, r]).start()
            return carry
        lax.fori_loop(0, R, body, 0, unroll=8)

    @pl.when((i == 0) & (tok_ref[0, 0, 0] >= 0))
    def _():
        start_rows(tok_ref, 0)

    @pl.when((i + 1 < pl.num_programs(0)) & (nxt_ref[0, 0, 0] >= 0))
    def _():
        start_rows(nxt_ref, 1 - slot)

    @pl.when(tok_ref[0, 0, 0] >= 0)
    def _():
        for r in range(R):
            row_copy(slot, r, 0).wait()
        o_ref[...] = buf[slot]

    @pl.when(tok_ref[0, 0, 0] < 0)
    def _():
        o_ref[...] = jnp.zeros(o_ref.shape, o_ref.dtype)


def _moe_gather(tok_buf, x):
    P = tok_buf.shape[0]
    D = x.shape[1]
    R = _tile(P, (512, 256, 128, 64))
    n = P // R
    toks = tok_buf.reshape(n, 1, R)
    return pl.pallas_call(
        functools.partial(_gather_kernel, R=R), name="moe_gather",
        out_shape=jax.ShapeDtypeStruct((P, D), x.dtype),
        grid=(n,),
        in_specs=[pl.BlockSpec((1, 1, R), lambda i: (i, 0, 0), memory_space=pltpu.SMEM),
                  pl.BlockSpec((1, 1, R), lambda i: (jnp.minimum(i + 1, n - 1), 0, 0), memory_space=pltpu.SMEM),
                  pl.BlockSpec(memory_space=pl.ANY)],
        out_specs=pl.BlockSpec((R, D), lambda i: (i, 0)),
        scratch_shapes=[pltpu.VMEM((2, R, D), x.dtype), pltpu.SemaphoreType.DMA((2,))],
        compiler_params=_cparams("arbitrary"),
    )(toks, toks, x)


def _expert_up_kernel(blk_ref, sub_ref, e_ref, nu_ref, xs_ref, wg_ref, wu_ref, h_ref):
    used = blk_ref[pl.program_id(0)] < nu_ref[0]

    @pl.when(used)
    def _():
        x = _unpack_bf16_pairs(xs_ref[...])
        gate = jnp.dot(x, wg_ref[0].astype(BF16), preferred_element_type=F32)
        up = jnp.dot(x, wu_ref[0].astype(BF16), preferred_element_type=F32)
        h_ref[...] = ((gate * jax.nn.sigmoid(gate)) * up).astype(h_ref.dtype)

    @pl.when(jnp.logical_not(used))
    def _():
        h_ref[...] = jnp.zeros(h_ref.shape, h_ref.dtype)


def _expert_down_kernel(blk_ref, sub_ref, e_ref, nu_ref, h_ref, wd_ref, o_ref):
    used = blk_ref[pl.program_id(0)] < nu_ref[0]

    @pl.when(used)
    def _():
        o_ref[...] = jnp.dot(h_ref[...], wd_ref[0].astype(BF16), preferred_element_type=F32)

    @pl.when(jnp.logical_not(used))
    def _():
        o_ref[...] = jnp.zeros(o_ref.shape, o_ref.dtype)


def _run_major(block_e, run_start, run_len, nsub):
    w = jnp.arange(block_e.shape[0] * nsub, dtype=I32)
    rs, rl = run_start[w // nsub], run_len[w // nsub]
    u = w - rs * nsub
    blk = rs + u % rl
    return blk, u // rl, block_e[blk]


def _moe_experts(block_e, run_start, run_len, nused, xs, w_gate, w_up, w_down, BM):
    P = xs.shape[0]
    NE, D, DE = w_gate.shape
    nblk = P // BM
    tf = _tile(DE, (512, 256, 128))
    tn = _tile(D, (2048, 1024, 512, 256, 128))
    up_items = _run_major(block_e, run_start, run_len, DE // tf)
    hidden = pl.pallas_call(
        _expert_up_kernel, name="moe_expert_up",
        grid_spec=pltpu.PrefetchScalarGridSpec(
            num_scalar_prefetch=4, grid=(nblk * (DE // tf),),
            in_specs=[pl.BlockSpec((BM, D // 2), lambda w, blk, sub, e, nu: (blk[w], 0)),
                      pl.BlockSpec((1, D, tf), lambda w, blk, sub, e, nu: (e[w], 0, sub[w])),
                      pl.BlockSpec((1, D, tf), lambda w, blk, sub, e, nu: (e[w], 0, sub[w]))],
            out_specs=pl.BlockSpec((BM, tf), lambda w, blk, sub, e, nu: (blk[w], sub[w]))),
        out_shape=jax.ShapeDtypeStruct((P, DE), BF16),
        compiler_params=_cparams("arbitrary"),
    )(*up_items, nused, xs, w_gate, w_up)
    down_items = _run_major(block_e, run_start, run_len, D // tn)
    return pl.pallas_call(
        _expert_down_kernel, name="moe_expert_down",
        grid_spec=pltpu.PrefetchScalarGridSpec(
            num_scalar_prefetch=4, grid=(nblk * (D // tn),),
            in_specs=[pl.BlockSpec((BM, DE), lambda w, blk, sub, e, nu: (blk[w], 0)),
                      pl.BlockSpec((1, DE, tn), lambda w, blk, sub, e, nu: (e[w], 0, sub[w]))],
            out_specs=pl.BlockSpec((BM, tn), lambda w, blk, sub, e, nu: (blk[w], sub[w]))),
        out_shape=jax.ShapeDtypeStruct((P, D), F32),
        compiler_params=_cparams("arbitrary"),
    )(*down_items, nused, hidden, w_down)


def _combine_kernel(pos_ref, nxt_ref, x_ref, g_ref, ew_ref, ys_hbm, o_ref, buf, sem, *, tc):
    i = pl.program_id(0)
    slot = i % 2

    def row_copy(s, r, jj, p):
        return pltpu.make_async_copy(ys_hbm.at[pl.ds(p, 1)], buf.at[s, jj, pl.ds(r, 1)], sem.at[s])

    def start_rows(pref, s):
        def body(r, carry):
            for jj in range(TOPK_EXPERTS):
                row_copy(s, r, jj, pref[0, 0, TOPK_EXPERTS * r + jj]).start()
            return carry
        lax.fori_loop(0, tc, body, 0, unroll=4)

    @pl.when(i == 0)
    def _():
        start_rows(pos_ref, 0)

    @pl.when(i + 1 < pl.num_programs(0))
    def _():
        start_rows(nxt_ref, 1 - slot)

    for r in range(tc):
        for jj in range(TOPK_EXPERTS):
            row_copy(slot, r, jj, 0).wait()
    moe = ew_ref[:, 0:1] * buf[slot, 0] + ew_ref[:, 1:2] * buf[slot, 1]
    o_ref[...] = x_ref[...] + g_ref[0] * moe


def _moe_combine(pos, x2d, S, g2, ew_all, tok_off, ys):
    T, D = x2d.shape
    tc = _tile(T, (256, 128, 64))
    assert tok_off % tc == 0
    n = T // tc
    g, rows, bidx = _mod_operand(g2, S, tc)
    posb = pos.reshape(n, 1, TOPK_EXPERTS * tc)
    pspec = lambda nxt: pl.BlockSpec((1, 1, TOPK_EXPERTS * tc), lambda i: (jnp.minimum(i + nxt, n - 1), 0, 0),
                                     memory_space=pltpu.SMEM)
    return pl.pallas_call(
        functools.partial(_combine_kernel, tc=tc), name="moe_combine",
        out_shape=jax.ShapeDtypeStruct((T, D), F32),
        grid=(n,),
        in_specs=[pspec(0), pspec(1),
                  pl.BlockSpec((tc, D), lambda i: (i, 0)),
                  pl.BlockSpec((1, rows, D), lambda i: (bidx(i), 0, 0)),
                  pl.BlockSpec((tc, LANES), lambda i: (i + tok_off // tc, 0)),
                  pl.BlockSpec(memory_space=pl.ANY)],
        out_specs=pl.BlockSpec((tc, D), lambda i: (i, 0)),
        scratch_shapes=[pltpu.VMEM((2, TOPK_EXPERTS, tc, D), F32), pltpu.SemaphoreType.DMA((2,))],
        compiler_params=_cparams("arbitrary"),
    )(posb, posb, x2d, g, ew_all, ys)


def _route_meta(eid, NE, BM):
    n_slots = eid.shape[0] * TOPK_EXPERTS
    slot_e = eid.reshape(-1)
    slot_id = jnp.arange(n_slots, dtype=I32)
    se, order = lax.sort((slot_e, slot_id), num_keys=1, is_stable=True)
    experts = jnp.arange(NE, dtype=I32)
    onehot = se[:, None] == experts[None, :]
    counts = jnp.sum(onehot, axis=0, dtype=I32)
    start = jnp.cumsum(counts) - counts
    padded = (counts + BM - 1) // BM * BM
    pend = jnp.cumsum(padded)
    pstart = pend - padded
    P = -(-n_slots // BM) * BM + NE * BM
    nblk = P // BM
    blk_row = jnp.arange(nblk, dtype=I32) * BM
    block_e = jnp.minimum(jnp.sum(pend[None, :] <= blk_row[:, None], axis=1, dtype=I32), NE - 1)
    nused = (pend[-1] // BM).astype(I32).reshape(1)
    blk_id = jnp.arange(nblk, dtype=I32)
    used = blk_row < pend[-1]
    run_start = jnp.where(used, pstart[block_e] // BM, blk_id).astype(I32)
    run_len = jnp.where(used, padded[block_e] // BM, 1).astype(I32)
    dest_sorted = slot_id + jnp.sum(jnp.where(onehot, (pstart - start)[None, :], 0), axis=1, dtype=I32)
    _, pos = lax.sort((order, dest_sorted), num_keys=1)
    src0 = start[block_e] + blk_row - pstart[block_e]
    order_pad = jnp.concatenate([order, jnp.zeros((BM,), I32)])
    runs = jax.vmap(lambda s: lax.dynamic_slice(order_pad, (s,), (BM,)))(jnp.clip(src0, 0, n_slots))
    src = src0[:, None] + jnp.arange(BM, dtype=I32)[None, :]
    valid = src < (start + counts)[block_e][:, None]
    tok_buf = jnp.where(valid, runs // TOPK_EXPERTS, 0)
    tok_buf = jnp.where((blk_row < pend[-1])[:, None], tok_buf, -1).reshape(P).astype(I32)
    return tok_buf, pos, (block_e, run_start, run_len, nused)


def _layer_front(x, mod, conv_prev, past, W, dims):
    B, S, D = x.shape
    T = B * S
    NH, KVH, hd, NIH, DI, NG, NE = dims
    sh1, sc1, g1, sh2, sc2, g2 = jnp.split(mod, 6, axis=-1)
    x2d = x.reshape(T, D)
    xm = _modulate(x2d, S, W["norm1_g"], sh1, sc1)
    hbc = _matmul(xm, W["w_in_conv"], BF16, name="inproj_conv")
    raw = _matmul(xm, W["w_in_attn"], F32, name="inproj_attn")
    gates = _matmul(xm, W["w_in_gate"], BF16, act="sigmoid", name="inproj_gate")
    q, k, v, ik, kb, vb, ikb, iq, iw = _attn_post(raw, W["q_norm_g"], W["k_norm_g"], W["ik_norm_g"],
                                                  NH, KVH, hd, NIH, DI)
    C = W["conv_w"].shape[1]
    conv_y, conv_state = _short_conv(hbc.reshape(B, S, 3 * C), conv_prev, W["conv_w"])
    r3 = lambda a: a.reshape(B, S, a.shape[-1])
    if past is None:
        k_all, v_all, ik_all = r3(kb), r3(vb), r3(ikb)
        L_true, causal = S, True
    else:
        k_past, v_past, ik_past = past
        Pn = k_past.shape[1]
        L_true, causal = Pn + S, False
        cat = lambda old, new: jnp.concatenate([old.reshape(B, Pn, -1).astype(BF16), r3(new)], axis=1)
        k_all, v_all, ik_all = cat(k_past, kb), cat(v_past, vb), cat(ik_past, ikb)
    attn = _attention(r3(q), r3(iq), r3(iw), k_all, v_all, ik_all, NH=NH, KVH=KVH, hd=hd, NIH=NIH, DI=DI,
                      topk=min(TOPK_MAX, L_true // 4), causal=causal, L_true=L_true)
    merged = _merge(conv_y.reshape(T, C), attn.reshape(T, NH * hd), W["w_conv_out"], W["w_attn_out"], gates)
    x_mid = _outproj(merged, W["w_out"], x2d, S, g1)
    outs = (k.reshape(B, S, KVH, hd), v.reshape(B, S, KVH, hd), ik.reshape(B, S, DI), conv_state)
    return x_mid, (sh2, sc2, g2), outs


def kernel(x_prompt, x_sample, cache_k, cache_v, cache_idx_k, state_conv, c_prompt, c_sample, norm1_g, norm2_g, w_ada, b_ada, w_in, conv_w, q_norm_g, k_norm_g, ik_norm_g, w_conv_out, w_attn_out, w_out, w_router_group, b_router_group, w_router_expert, b_router_expert, w_gate, w_up, w_down):
    Bp, Sp, D = x_prompt.shape
    Bs, Ss, _ = x_sample.shape
    CW, C = conv_w.shape
    hd = q_norm_g.shape[0]
    DI = ik_norm_g.shape[0]
    KVH = cache_k.shape[2]
    NH = w_attn_out.shape[0] // hd
    NG = b_router_group.shape[0]
    NE = b_router_expert.shape[0]
    off_q = 3 * C
    off_iq = off_q + NH * hd + 2 * KVH * hd
    off_gate = w_in.shape[1] - 2 * D
    NIH = (off_gate - off_iq - DI) // (DI + 1)
    dims = (NH, KVH, hd, NIH, DI, NG, NE)

    na = off_gate - off_q
    na_pad = -(-na // LANES) * LANES
    rpad = LANES - NG - NE
    W = {
        "norm1_g": norm1_g, "norm2_g": norm2_g, "conv_w": conv_w,
        "q_norm_g": q_norm_g, "k_norm_g": k_norm_g, "ik_norm_g": ik_norm_g,
        "w_in_conv": w_in[:, :off_q].astype(BF16),
        "w_in_attn": jnp.pad(w_in[:, off_q:off_gate], ((0, 0), (0, na_pad - na))).astype(BF16),
        "w_in_gate": w_in[:, off_gate:].astype(BF16),
        "w_conv_out": w_conv_out.astype(BF16), "w_attn_out": w_attn_out.astype(BF16),
        "w_out": w_out.astype(BF16),
        "w_router": jnp.pad(jnp.concatenate([w_router_group, w_router_expert], axis=1), ((0, 0), (0, rpad))),
        "b_router": jnp.pad(jnp.concatenate([b_router_group, b_router_expert]), (0, rpad)).reshape(1, LANES),
    }

    nb = Bp + Bs
    c_all = jnp.pad(jnp.concatenate([c_prompt, c_sample], axis=0), ((0, -nb % 8), (0, 0)))
    mod = _ada(c_all, w_ada, b_ada)

    conv_zero = jnp.zeros((Bp, CW - 1, C), F32)
    xmid_p, (sh2_p, sc2_p, g2_p), outs_p = _layer_front(x_prompt, mod[:Bp], conv_zero, None, W, dims)
    xmid_s, (sh2_s, sc2_s, g2_s), outs_s = _layer_front(
        x_sample, mod[Bp:nb], state_conv, (cache_k, cache_v, cache_idx_k), W, dims)

    Tp = Bp * Sp
    xm2, eid, ew = _router([(xmid_p, Sp, sh2_p, sc2_p), (xmid_s, Ss, sh2_s, sc2_s)],
                           norm2_g, W["w_router"], W["b_router"], NG, NE)
    tok_buf, pos, blocks = _route_meta(eid[:, :TOPK_EXPERTS], NE, MOE_BM)
    xs = _moe_gather(tok_buf, xm2)
    ys = _moe_experts(*blocks, xs, w_gate, w_up, w_down, MOE_BM)
    y_p = _moe_combine(pos[:TOPK_EXPERTS * Tp], xmid_p, Sp, g2_p, ew, 0, ys).reshape(Bp, Sp, D)
    y_s = _moe_combine(pos[TOPK_EXPERTS * Tp:], xmid_s, Ss, g2_s, ew, Tp, ys).reshape(Bs, Ss, D)

    k_p, v_p, ik_p, conv_p = outs_p
    k_s, v_s, ik_s, conv_s = outs_s
    return (y_p, y_s, k_p, v_p, ik_p, conv_p, k_s, v_s, ik_s, conv_s)
```

```python
import functools

import jax
import jax.numpy as jnp
from jax import lax
from jax.experimental import pallas as pl
from jax.experimental.pallas import tpu as pltpu

F32 = jnp.float32
BF16 = jnp.bfloat16
I32 = jnp.int32

CHUNK = 64
EPS = 1e-6
TOPK_MAX = 256
TOPK_EXPERTS = 2
LANES = 128
VMEM_LIMIT = 56 * 1024 * 1024
INT_MIN = -2 ** 31
NEG = -1e30
LOG2E = 1.4426950408889634
MOE_BM = 512
ATTN_KEY_TILE = 256
IDX_HEADS_PER_DOT = 16
NT_DIMS = (((1,), (1,)), ((), ()))


def _tile(n, prefs):
    for t in prefs:
        if n % t == 0:
            return t
    return n


def _cparams(*sem):
    return pltpu.CompilerParams(dimension_semantics=sem, vmem_limit_bytes=VMEM_LIMIT)


def _mod_operand(m, S, tm):
    B, D = m.shape
    if S % tm == 0:
        return m.reshape(B, 1, D), 1, (lambda i: (i * tm) // S)
    assert tm % S == 0
    full = jnp.broadcast_to(m[:, None, :], (B, S, D)).reshape((B * S) // tm, tm, D)
    return full, tm, (lambda i: i)


def _ada_kernel(c_ref, w_ref, b_ref, o_ref):
    o_ref[...] = jnp.dot(c_ref[...], w_ref[...], preferred_element_type=F32,
                         precision=lax.Precision.HIGHEST) + b_ref[...]


def _ada(c, w, b):
    M, D = c.shape
    N = w.shape[1]
    tn = _tile(N, (512, 256, 128))
    return pl.pallas_call(
        _ada_kernel, name="ada",
        out_shape=jax.ShapeDtypeStruct((M, N), F32),
        grid=(N // tn,),
        in_specs=[pl.BlockSpec((M, D), lambda j: (0, 0)),
                  pl.BlockSpec((D, tn), lambda j: (0, j)),
                  pl.BlockSpec((1, tn), lambda j: (0, j))],
        out_specs=pl.BlockSpec((M, tn), lambda j: (0, j)),
        compiler_params=_cparams("arbitrary"),
    )(c, w, b.reshape(1, N))


def _modulated(x, g, scale, shift):
    ms = jnp.mean(x * x, axis=-1, keepdims=True)
    return (x * lax.rsqrt(ms + EPS)) * g * (1.0 + scale) + shift


def _modulate_kernel(x_ref, g_ref, sh_ref, sc_ref, o_ref):
    o_ref[...] = _modulated(x_ref[...], g_ref[...], sc_ref[0], sh_ref[0]).astype(o_ref.dtype)


def _modulate(x2d, S, g, shift, scale):
    T, D = x2d.shape
    tm = _tile(S, (512, 256, 128, 64))
    sh, rows, bidx = _mod_operand(shift, S, tm)
    sc, _, _ = _mod_operand(scale, S, tm)
    mspec = pl.BlockSpec((1, rows, D), lambda i: (bidx(i), 0, 0))
    return pl.pallas_call(
        _modulate_kernel, name="modulate",
        out_shape=jax.ShapeDtypeStruct((T, D), BF16),
        grid=(T // tm,),
        in_specs=[pl.BlockSpec((tm, D), lambda i: (i, 0)),
                  pl.BlockSpec((1, D), lambda i: (0, 0)), mspec, mspec],
        out_specs=pl.BlockSpec((tm, D), lambda i: (i, 0)),
        compiler_params=_cparams("arbitrary"),
    )(x2d, g.reshape(1, D), sh, sc)


def _mm_kernel(a_ref, b_ref, o_ref, *, act):
    acc = jnp.dot(a_ref[...], b_ref[...], preferred_element_type=F32)
    if act == "sigmoid":
        acc = jax.nn.sigmoid(acc)
    o_ref[...] = acc.astype(o_ref.dtype)


def _matmul(a, b, out_dtype, act="none", name="matmul"):
    M, K = a.shape
    N = b.shape[1]
    tm = _tile(M, (1024, 512, 256, 128))
    tn = _tile(N, (512, 384, 256, 128))
    return pl.pallas_call(
        functools.partial(_mm_kernel, act=act), name=name,
        out_shape=jax.ShapeDtypeStruct((M, N), out_dtype),
        grid=(M // tm, N // tn),
        in_specs=[pl.BlockSpec((tm, K), lambda i, j: (i, 0)),
                  pl.BlockSpec((K, tn), lambda i, j: (0, j))],
        out_specs=pl.BlockSpec((tm, tn), lambda i, j: (i, j)),
        compiler_params=_cparams("arbitrary", "arbitrary"),
    )(a, b)


def _attn_post_kernel(raw_ref, qg_ref, kg_ref, ikg_ref,
                      q_ref, k_ref, v_ref, ik_ref, kb_ref, vb_ref, ikb_ref, iq_ref, iw_ref,
                      *, NH, KVH, hd, NIH, DI):
    def headnorm(col, g):
        seg = raw_ref[:, col:col + hd]
        ms = jnp.mean(seg * seg, axis=-1, keepdims=True)
        return seg * lax.rsqrt(ms + EPS) * g

    qg = qg_ref[...] * (LOG2E * hd ** -0.5)
    for h in range(NH):
        q_ref[:, h * hd:(h + 1) * hd] = (headnorm(h * hd, 1.0) * qg).astype(q_ref.dtype)
    off_k = NH * hd
    off_v = off_k + KVH * hd
    for h in range(KVH):
        kn = headnorm(off_k + h * hd, kg_ref[...])
        k_ref[:, h * hd:(h + 1) * hd] = kn
        kb_ref[:, h * hd:(h + 1) * hd] = kn.astype(kb_ref.dtype)
    v = raw_ref[:, off_v:off_v + KVH * hd]
    v_ref[...] = v
    vb_ref[...] = v.astype(vb_ref.dtype)
    off_iq = off_v + KVH * hd
    off_ik = off_iq + NIH * DI
    off_iw = off_ik + DI
    iq_ref[...] = (raw_ref[:, off_iq:off_ik] * (DI ** -0.5)).astype(iq_ref.dtype)
    ikr = raw_ref[:, off_ik:off_iw]
    ikn = ikr * lax.rsqrt(jnp.mean(ikr * ikr, axis=-1, keepdims=True) + EPS) * ikg_ref[...]
    ik_ref[...] = ikn
    ikb_ref[...] = ikn.astype(ikb_ref.dtype)
    iw_ref[...] = raw_ref[:, off_iw:off_iw + NIH] * (NIH ** -0.5)


def _attn_post(raw, q_norm_g, k_norm_g, ik_norm_g, NH, KVH, hd, NIH, DI):
    T, NA = raw.shape
    tr = _tile(T, (256, 128, 64))
    outs = [(NH * hd, BF16), (KVH * hd, F32), (KVH * hd, F32), (DI, F32),
            (KVH * hd, BF16), (KVH * hd, BF16), (DI, BF16), (NIH * DI, BF16), (NIH, F32)]
    return pl.pallas_call(
        functools.partial(_attn_post_kernel, NH=NH, KVH=KVH, hd=hd, NIH=NIH, DI=DI), name="attn_post",
        out_shape=[jax.ShapeDtypeStruct((T, n), dt) for n, dt in outs],
        grid=(T // tr,),
        in_specs=[pl.BlockSpec((tr, NA), lambda i: (i, 0)),
                  pl.BlockSpec((1, hd), lambda i: (0, 0)),
                  pl.BlockSpec((1, hd), lambda i: (0, 0)),
                  pl.BlockSpec((1, DI), lambda i: (0, 0))],
        out_specs=[pl.BlockSpec((tr, n), lambda i: (i, 0)) for n, _ in outs],
        compiler_params=_cparams("arbitrary"),
    )(raw, q_norm_g.reshape(1, hd), k_norm_g.reshape(1, hd), ik_norm_g.reshape(1, DI))


def _conv_kernel(h_ref, b_ref, c_ref, w_ref, prev_ref, y_ref, st_ref, ubuf, *, ts, CW):
    lo = 8 - (CW - 1)

    @pl.when(pl.program_id(1) == 0)
    def _():
        ubuf[lo:8, :] = prev_ref[0]

    @pl.when(pl.program_id(1) > 0)
    def _():
        ubuf[lo:8, :] = ubuf[ts + lo:ts + 8, :]

    ubuf[8:8 + ts, :] = c_ref[0].astype(F32) * h_ref[0].astype(F32)
    y = w_ref[0:1, :] * ubuf[lo:lo + ts, :]
    for j in range(1, CW):
        y = y + w_ref[j:j + 1, :] * ubuf[lo + j:lo + j + ts, :]
    y_ref[0] = (b_ref[0].astype(F32) * y).astype(y_ref.dtype)
    st_ref[0] = ubuf[ts + lo:ts + 8, :]


def _short_conv(hbc, conv_prev, conv_w):
    B, S, C3 = hbc.shape
    CW, C = conv_w.shape
    ts = _tile(S, (512, 256, 128, 64))
    col = lambda k: pl.BlockSpec((1, ts, C), lambda b, s: (b, s, k))
    return pl.pallas_call(
        functools.partial(_conv_kernel, ts=ts, CW=CW), name="short_conv",
        out_shape=[jax.ShapeDtypeStruct((B, S, C), BF16), jax.ShapeDtypeStruct((B, CW - 1, C), F32)],
        grid=(B, S // ts),
        in_specs=[col(0), col(1), col(2),
                  pl.BlockSpec((CW, C), lambda b, s: (0, 0)),
                  pl.BlockSpec((1, CW - 1, C), lambda b, s: (b, 0, 0))],
        out_specs=[pl.BlockSpec((1, ts, C), lambda b, s: (b, s, 0)),
                   pl.BlockSpec((1, CW - 1, C), lambda b, s: (b, 0, 0))],
        scratch_shapes=[pltpu.VMEM((ts + 8, C), F32)],
        compiler_params=_cparams("arbitrary", "arbitrary"),
    )(hbc, hbc, hbc, conv_w, conv_prev)


def _attn_kernel(q_ref, iq_ref, iwt_ref, k_ref, vt_ref, ik_ref, o_ref, keys_ref, bias_sc, m_sc, l_sc, acc_sc,
                 *, QB, TL, NH, KVH, hd, NIH, DI, topk, causal, L_true, Lp):
    i = pl.program_id(1)
    rep = NH // KVH
    if causal:
        qrow = i * QB + lax.broadcasted_iota(I32, (1, QB), 1)
        vis = (qrow // CHUNK + 1) * CHUNK
        nkt = ((i + 1) * QB + TL - 1) // TL
    else:
        vis = jnp.full((1, QB), L_true, I32)
        nkt = Lp // TL

    def score_tile(t, carry):
        off = pl.multiple_of(t * TL, TL)
        ikt = ik_ref[0, pl.ds(off, TL), :]
        acc = jnp.zeros((TL, QB), F32)
        for h0 in range(0, NIH, IDX_HEADS_PER_DOT):
            nh = min(IDX_HEADS_PER_DOT, NIH - h0)
            d = lax.dot_general(ikt, iq_ref[0, 0, h0 * QB:(h0 + nh) * QB, :], NT_DIMS, preferred_element_type=F32)
            for h in range(h0, h0 + nh):
                acc = acc + iwt_ref[0, h:h + 1, :] * jnp.maximum(d[:, (h - h0) * QB:(h - h0 + 1) * QB], 0.0)
        bits = pltpu.bitcast(acc, I32)
        key = bits ^ ((bits >> 31) & jnp.int32(0x7FFFFFFF))
        kid = off + lax.broadcasted_iota(I32, (TL, 1), 0)
        keys_ref[pl.ds(off, TL), :] = jnp.where(kid < vis, key, jnp.int32(INT_MIN))
        return carry

    lax.fori_loop(0, nkt, score_tile, 0)

    def count_ge(cand):
        def body(t, c):
            off = pl.multiple_of(t * TL, TL)
            hit = jnp.where(keys_ref[pl.ds(off, TL), :] >= cand, 1, 0).astype(I32)
            return c + jnp.sum(hit.reshape(TL // 8, 8, QB), axis=0)
        c = lax.fori_loop(0, nkt, body, jnp.zeros((8, QB), I32))
        return jnp.sum(c, axis=0, keepdims=True)

    def bit_step(bi, thr):
        cand = thr + (jnp.int32(1) << (31 - bi))
        return jnp.where(count_ge(cand) >= topk, cand, thr)

    thr = lax.fori_loop(0, 32, bit_step, jnp.full((1, QB), INT_MIN, I32))
    thr = jnp.maximum(thr, jnp.int32(INT_MIN + 1))

    def tie_search(_):
        need = topk - count_ge(thr + 1)

        def count_eq_below(cand):
            def body(t, c):
                off = pl.multiple_of(t * TL, TL)
                kid = off + lax.broadcasted_iota(I32, (TL, 1), 0)
                hit = jnp.where((keys_ref[pl.ds(off, TL), :] == thr) & (kid < cand), 1, 0).astype(I32)
                return c + jnp.sum(hit.reshape(TL // 8, 8, QB), axis=0)
            c = lax.fori_loop(0, nkt, body, jnp.zeros((8, QB), I32))
            return jnp.sum(c, axis=0, keepdims=True)

        def idx_step(bi, lo):
            cand = lo + (jnp.int32(1) << (IDX_BITS - 1 - bi))
            return jnp.where(count_eq_below(cand) < need, cand, lo)

        return lax.fori_loop(0, IDX_BITS, idx_step, jnp.zeros((1, QB), I32))

    IDX_BITS = max(1, Lp.bit_length())
    tie_idx = lax.cond(jnp.max(count_ge(thr)) > topk, tie_search,
                       lambda _: jnp.full((1, QB), Lp, I32), 0)

    m_sc[...] = jnp.full(m_sc.shape, NEG, F32)
    l_sc[...] = jnp.zeros(l_sc.shape, F32)
    acc_sc[...] = jnp.zeros(acc_sc.shape, F32)

    def kv_tile(t, carry):
        off = pl.multiple_of(t * TL, TL)
        kid = off + lax.broadcasted_iota(I32, (TL, 1), 0)
        kt_keys = keys_ref[pl.ds(off, TL), :]
        keep = (kt_keys > thr) | ((kt_keys == thr) & (kid <= tie_idx))
        bias_sc[...] = jnp.where(keep, 0.0, NEG)
        for g in range(KVH):
            kt = k_ref[0, pl.ds(off, TL), g * hd:(g + 1) * hd]
            vt = vt_ref[0, g * hd:(g + 1) * hd, pl.ds(off, TL)]
            for r in range(rep):
                h = g * rep + r
                s = lax.dot_general(kt, q_ref[0, :, h * hd:(h + 1) * hd], NT_DIMS,
                                    preferred_element_type=F32) + bias_sc[...]
                m_old = m_sc[h:h + 1, :]
                m_new = jnp.maximum(m_old, jnp.max(s, axis=0, keepdims=True))
                alpha = jnp.exp2(m_old - m_new)
                p = jnp.exp2(s - m_new)
                l_sc[h:h + 1, :] = alpha * l_sc[h:h + 1, :] + jnp.sum(p, axis=0, keepdims=True)
                acc_sc[h] = alpha * acc_sc[h] + jnp.dot(vt, p.astype(vt.dtype), preferred_element_type=F32)
                m_sc[h:h + 1, :] = m_new
        return carry

    lax.fori_loop(0, nkt, kv_tile, 0)
    for h in range(NH):
        o = acc_sc[h] / l_sc[h:h + 1, :]
        o_ref[0, :, h * hd:(h + 1) * hd] = o.T.astype(o_ref.dtype)


def _attention(q, iq, iw, k_all, v_all, ik_all, *, NH, KVH, hd, NIH, DI, topk, causal, L_true):
    B, S0, _ = q.shape
    QB = LANES
    S = -(-S0 // QB) * QB
    TL = ATTN_KEY_TILE
    Lp = -(-k_all.shape[1] // TL) * TL
    pad_s = lambda a: jnp.pad(a, ((0, 0), (0, S - S0), (0, 0)))
    pad_l = lambda a: jnp.pad(a, ((0, 0), (0, Lp - a.shape[1]), (0, 0)))
    q, iq, iw = pad_s(q), pad_s(iq), pad_s(iw)
    k_all, v_all, ik_all = pad_l(k_all), pad_l(v_all), pad_l(ik_all)
    iq_heads = iq.reshape(B, S // QB, QB, NIH, DI).transpose(0, 1, 3, 2, 4).reshape(B, S // QB, NIH * QB, DI)
    qspec = lambda n: pl.BlockSpec((1, QB, n), lambda b, i: (b, i, 0))
    kspec = lambda n: pl.BlockSpec((1, Lp, n), lambda b, i: (b, 0, 0))
    out = pl.pallas_call(
        functools.partial(_attn_kernel, QB=QB, TL=TL, NH=NH, KVH=KVH, hd=hd, NIH=NIH, DI=DI,
                          topk=topk, causal=causal, L_true=L_true, Lp=Lp), name="sparse_attn",
        out_shape=jax.ShapeDtypeStruct((B, S, NH * hd), BF16),
        grid=(B, S // QB),
        in_specs=[qspec(NH * hd),
                  pl.BlockSpec((1, 1, NIH * QB, DI), lambda b, i: (b, i, 0, 0)),
                  pl.BlockSpec((1, NIH, QB), lambda b, i: (b, 0, i)),
                  kspec(KVH * hd),
                  pl.BlockSpec((1, KVH * hd, Lp), lambda b, i: (b, 0, 0)),
                  kspec(DI)],
        out_specs=qspec(NH * hd),
        scratch_shapes=[pltpu.VMEM((Lp, QB), I32), pltpu.VMEM((TL, QB), F32), pltpu.VMEM((NH, QB), F32),
                        pltpu.VMEM((NH, QB), F32), pltpu.VMEM((NH, hd, QB), F32)],
        compiler_params=_cparams("arbitrary", "arbitrary"),
    )(q, iq_heads, jnp.swapaxes(iw, 1, 2), k_all, jnp.swapaxes(v_all, 1, 2), ik_all)
    return out[:, :S0]


def _merge_kernel(cy_ref, at_ref, wc_ref, wa_ref, gc_ref, ga_ref, o_ref):
    a = jnp.dot(cy_ref[...], wc_ref[...], preferred_element_type=F32)
    b = jnp.dot(at_ref[...], wa_ref[...], preferred_element_type=F32)
    o_ref[...] = (gc_ref[...].astype(F32) * a + ga_ref[...].astype(F32) * b).astype(o_ref.dtype)


def _merge(conv_y, attn, w_conv_out, w_attn_out, gates):
    T, C = conv_y.shape
    A = attn.shape[1]
    D = w_conv_out.shape[1]
    tm = _tile(T, (1024, 512, 256, 128))
    tn = _tile(D, (512, 256, 128))
    nj = D // tn
    return pl.pallas_call(
        _merge_kernel, name="merge",
        out_shape=jax.ShapeDtypeStruct((T, D), BF16),
        grid=(T // tm, nj),
        in_specs=[pl.BlockSpec((tm, C), lambda i, j: (i, 0)),
                  pl.BlockSpec((tm, A), lambda i, j: (i, 0)),
                  pl.BlockSpec((C, tn), lambda i, j: (0, j)),
                  pl.BlockSpec((A, tn), lambda i, j: (0, j)),
                  pl.BlockSpec((tm, tn), lambda i, j: (i, j)),
                  pl.BlockSpec((tm, tn), lambda i, j: (i, j + nj))],
        out_specs=pl.BlockSpec((tm, tn), lambda i, j: (i, j)),
        compiler_params=_cparams("arbitrary", "arbitrary"),
    )(conv_y, attn, w_conv_out, w_attn_out, gates, gates)


def _outproj_kernel(a_ref, w_ref, x_ref, g_ref, o_ref):
    acc = jnp.dot(a_ref[...], w_ref[...], preferred_element_type=F32)
    o_ref[...] = x_ref[...] + g_ref[0] * acc


def _outproj(merged, w_out, x2d, S, g1):
    T, D = x2d.shape
    K = merged.shape[1]
    tm = _tile(T, (1024, 512, 256, 128))
    tn = _tile(D, (512, 256, 128))
    g, rows, bidx = _mod_operand(g1, S, tm)
    return pl.pallas_call(
        _outproj_kernel, name="outproj",
        out_shape=jax.ShapeDtypeStruct((T, D), F32),
        grid=(T // tm, D // tn),
        in_specs=[pl.BlockSpec((tm, K), lambda i, j: (i, 0)),
                  pl.BlockSpec((K, tn), lambda i, j: (0, j)),
                  pl.BlockSpec((tm, tn), lambda i, j: (i, j)),
                  pl.BlockSpec((1, rows, tn), lambda i, j: (bidx(i), 0, j))],
        out_specs=pl.BlockSpec((tm, tn), lambda i, j: (i, j)),
        compiler_params=_cparams("arbitrary", "arbitrary"),
    )(merged, w_out, x2d, g)


def _pack_bf16_pairs(x):
    half = x.shape[-1] // 2
    lo = pltpu.bitcast(x[:, :half].astype(BF16).astype(F32), I32)
    hi = pltpu.bitcast(x[:, half:].astype(BF16).astype(F32), I32)
    return ((lo >> 16) & jnp.int32(0xFFFF)) | (hi & jnp.int32(-65536))


def _unpack_bf16_pairs(w):
    lo = pltpu.bitcast(w << 16, F32).astype(BF16)
    hi = pltpu.bitcast(w & jnp.int32(-65536), F32).astype(BF16)
    return jnp.concatenate([lo, hi], axis=-1)


def _router_kernel(*refs, NG, NE, nblk):
    npath = len(nblk)
    shared = refs[3 * npath:]
    i = pl.program_id(0)
    lo = 0
    for p in range(npath):
        @pl.when((i >= lo) & (i < lo + nblk[p]))
        def _(path_refs=refs[3 * p:3 * p + 3]):
            _route_rows(*path_refs, *shared, NG=NG, NE=NE)
        lo += nblk[p]


def _route_rows(x_ref, sh_ref, sc_ref, g_ref, wr_ref, br_ref, xm_ref, eid_ref, ew_ref, *, NG, NE):
    xm = _modulated(x_ref[...], g_ref[...], sc_ref[0], sh_ref[0])
    xm_ref[...] = _pack_bf16_pairs(xm)
    logits = jnp.dot(xm, wr_ref[...], preferred_element_type=F32,
                     precision=lax.Precision.HIGHEST) + br_ref[...]
    tm = logits.shape[0]
    epg = NE // NG
    lane = lax.broadcasted_iota(I32, (tm, LANES), 1)
    ninf = -jnp.inf

    def first_argmax(vals):
        mx = jnp.max(vals, axis=1, keepdims=True)
        return mx, jnp.min(jnp.where(vals == mx, lane, LANES), axis=1, keepdims=True)

    gmask = lane < NG
    gmax, g_sel = first_argmax(jnp.where(gmask, logits, ninf))
    g_w = 1.0 / jnp.sum(jnp.where(gmask, jnp.exp(logits - gmax), 0.0), axis=1, keepdims=True)
    emask = (lane >= NG) & (lane < NG + NE) & ((lane - NG) // epg == g_sel)
    el = jnp.where(emask, logits, ninf)
    m1, i1 = first_argmax(el)
    m2, i2 = first_argmax(jnp.where(lane == i1, ninf, el))
    r = jnp.exp(m2 - m1)
    w1 = g_w / (1.0 + r)
    w2 = w1 * r
    eid_ref[...] = jnp.where(lane == 0, i1 - NG, jnp.where(lane == 1, i2 - NG, 0))
    ew_ref[...] = jnp.where(lane == 0, w1, jnp.where(lane == 1, w2, 0.0))


def _router(paths, g, wr, br, NG, NE):
    D = paths[0][0].shape[1]
    tm = next(t for t in (256, 128, 64, 32, 16, 8)
              if all(x.shape[0] % t == 0 and (S % t == 0 or t % S == 0) for x, S, _, _ in paths))
    nblk = tuple(x.shape[0] // tm for x, _, _, _ in paths)
    T = sum(x.shape[0] for x, _, _, _ in paths)
    operands, in_specs, lo = [], [], 0
    for (x, S, shift, scale), nb in zip(paths, nblk):
        sh, rows, bidx = _mod_operand(shift, S, tm)
        sc, _, _ = _mod_operand(scale, S, tm)
        local = lambda i, lo=lo, nb=nb: jnp.clip(i - lo, 0, nb - 1)
        mspec = pl.BlockSpec((1, rows, D), lambda i, local=local, bidx=bidx: (bidx(local(i)), 0, 0))
        operands += [x, sh, sc]
        in_specs += [pl.BlockSpec((tm, D), lambda i, local=local: (local(i), 0)), mspec, mspec]
        lo += nb
    rowspec = lambda n: pl.BlockSpec((tm, n), lambda i: (i, 0))
    return pl.pallas_call(
        functools.partial(_router_kernel, NG=NG, NE=NE, nblk=nblk), name="router",
        out_shape=[jax.ShapeDtypeStruct((T, D // 2), I32), jax.ShapeDtypeStruct((T, LANES), I32),
                   jax.ShapeDtypeStruct((T, LANES), F32)],
        grid=(sum(nblk),),
        in_specs=in_specs + [pl.BlockSpec((1, D), lambda i: (0, 0)),
                             pl.BlockSpec((D, LANES), lambda i: (0, 0)), pl.BlockSpec((1, LANES), lambda i: (0, 0))],
        out_specs=[rowspec(D // 2), rowspec(LANES), rowspec(LANES)],
        compiler_params=_cparams("arbitrary"),
    )(*operands, g.reshape(1, D), wr, br)


def _gather_kernel(tok_ref, nxt_ref, x_hbm, o_ref, buf, sem, *, R):
    i = pl.program_id(0)
    slot = i % 2

    def row_copy(s, r, t):
        return pltpu.make_async_copy(x_hbm.at[pl.ds(t, 1)], buf.at[s, pl.ds(r, 1)], sem.at[s])

    def start_rows(tref, s):
        def body(r, carry):
            row_copy(s, r, tref[0, 0, r]).start()
            return carry
        lax.fori_loop(0, R, body, 0, unroll=8)

    @pl.when((i == 0) & (tok_ref[0, 0, 0] >= 0))
    def _():
        start_rows(tok_ref, 0)

    @pl.when((i + 1 < pl.num_programs(0)) & (nxt_ref[0, 0, 0] >= 0))
    def _():
        start_rows(nxt_ref, 1 - slot)

    @pl.when(tok_ref[0, 0, 0] >= 0)
    def _():
        for r in range(R):
            row_copy(slot, r, 0).wait()
        o_ref[...] = buf[slot]

    @pl.when(tok_ref[0, 0, 0] < 0)
    def _():
        o_ref[...] = jnp.zeros(o_ref.shape, o_ref.dtype)


def _moe_gather(tok_buf, x):
    P = tok_buf.shape[0]
    D = x.shape[1]
    R = _tile(P, (512, 256, 128, 64))
    n = P // R
    toks = tok_buf.reshape(n, 1, R)
    return pl.pallas_call(
        functools.partial(_gather_kernel, R=R), name="moe_gather",
        out_shape=jax.ShapeDtypeStruct((P, D), x.dtype),
        grid=(n,),
        in_specs=[pl.BlockSpec((1, 1, R), lambda i: (i, 0, 0), memory_space=pltpu.SMEM),
                  pl.BlockSpec((1, 1, R), lambda i: (jnp.minimum(i + 1, n - 1), 0, 0), memory_space=pltpu.SMEM),
                  pl.BlockSpec(memory_space=pl.ANY)],
        out_specs=pl.BlockSpec((R, D), lambda i: (i, 0)),
        scratch_shapes=[pltpu.VMEM((2, R, D), x.dtype), pltpu.SemaphoreType.DMA((2,))],
        compiler_params=_cparams("arbitrary"),
    )(toks, toks, x)


def _new_weight_chunk(sub_ref, e_ref):
    w = pl.program_id(0)
    prev = jnp.maximum(w - 1, 0)
    return (w == 0) | (e_ref[w] != e_ref[prev]) | (sub_ref[w] != sub_ref[prev])


def _expert_up_kernel(blk_ref, sub_ref, e_ref, nu_ref, xs_ref, wg_ref, wu_ref, h_ref, wg_sc, wu_sc):
    used = blk_ref[pl.program_id(0)] < nu_ref[0]

    @pl.when(_new_weight_chunk(sub_ref, e_ref))
    def _():
        wg_sc[...] = wg_ref[0].astype(BF16)
        wu_sc[...] = wu_ref[0].astype(BF16)

    @pl.when(used)
    def _():
        x = _unpack_bf16_pairs(xs_ref[...])
        gate = jnp.dot(x, wg_sc[...], preferred_element_type=F32)
        up = jnp.dot(x, wu_sc[...], preferred_element_type=F32)
        h_ref[...] = ((gate * jax.nn.sigmoid(gate)) * up).astype(h_ref.dtype)

    @pl.when(jnp.logical_not(used))
    def _():
        h_ref[...] = jnp.zeros(h_ref.shape, h_ref.dtype)


def _expert_down_kernel(blk_ref, sub_ref, e_ref, nu_ref, h_ref, wd_ref, o_ref, wd_sc):
    used = blk_ref[pl.program_id(0)] < nu_ref[0]

    @pl.when(_new_weight_chunk(sub_ref, e_ref))
    def _():
        wd_sc[...] = wd_ref[0].astype(BF16)

    @pl.when(used)
    def _():
        o_ref[...] = jnp.dot(h_ref[...], wd_sc[...], preferred_element_type=F32)

    @pl.when(jnp.logical_not(used))
    def _():
        o_ref[...] = jnp.zeros(o_ref.shape, o_ref.dtype)


def _run_major(block_e, run_start, run_len, nsub):
    w = jnp.arange(block_e.shape[0] * nsub, dtype=I32)
    rs, rl = run_start[w // nsub], run_len[w // nsub]
    u = w - rs * nsub
    blk = rs + u % rl
    return blk, u // rl, block_e[blk]


def _moe_experts(block_e, run_start, run_len, nused, xs, w_gate, w_up, w_down, BM):
    P = xs.shape[0]
    NE, D, DE = w_gate.shape
    nblk = P // BM
    tf = _tile(DE, (512, 256, 128))
    tn = _tile(D, (2048, 1024, 512, 256, 128))
    up_items = _run_major(block_e, run_start, run_len, DE // tf)
    hidden = pl.pallas_call(
        _expert_up_kernel, name="moe_expert_up",
        grid_spec=pltpu.PrefetchScalarGridSpec(
            num_scalar_prefetch=4, grid=(nblk * (DE // tf),),
            in_specs=[pl.BlockSpec((BM, D // 2), lambda w, blk, sub, e, nu: (blk[w], 0)),
                      pl.BlockSpec((1, D, tf), lambda w, blk, sub, e, nu: (e[w], 0, sub[w])),
                      pl.BlockSpec((1, D, tf), lambda w, blk, sub, e, nu: (e[w], 0, sub[w]))],
            out_specs=pl.BlockSpec((BM, tf), lambda w, blk, sub, e, nu: (blk[w], sub[w])),
            scratch_shapes=[pltpu.VMEM((D, tf), BF16), pltpu.VMEM((D, tf), BF16)]),
        out_shape=jax.ShapeDtypeStruct((P, DE), BF16),
        compiler_params=_cparams("arbitrary"),
    )(*up_items, nused, xs, w_gate, w_up)
    down_items = _run_major(block_e, run_start, run_len, D // tn)
    return pl.pallas_call(
        _expert_down_kernel, name="moe_expert_down",
        grid_spec=pltpu.PrefetchScalarGridSpec(
            num_scalar_prefetch=4, grid=(nblk * (D // tn),),
            in_specs=[pl.BlockSpec((BM, DE), lambda w, blk, sub, e, nu: (blk[w], 0)),
                      pl.BlockSpec((1, DE, tn), lambda w, blk, sub, e, nu: (e[w], 0, sub[w]))],
            out_specs=pl.BlockSpec((BM, tn), lambda w, blk, sub, e, nu: (blk[w], sub[w])),
            scratch_shapes=[pltpu.VMEM((DE, tn), BF16)]),
        out_shape=jax.ShapeDtypeStruct((P, D), F32),
        compiler_params=_cparams("arbitrary"),
    )(*down_items, nused, hidden, w_down)


def _combine_kernel(pos_ref, nxt_ref, x_ref, g_ref, ew_ref, ys_hbm, o_ref, buf, sem, *, tc):
    i = pl.program_id(0)
    slot = i % 2

    def row_copy(s, r, jj, p):
        return pltpu.make_async_copy(ys_hbm.at[pl.ds(p, 1)], buf.at[s, jj, pl.ds(r, 1)], sem.at[s])

    def start_rows(pref, s):
        def body(r, carry):
            for jj in range(TOPK_EXPERTS):
                row_copy(s, r, jj, pref[0, 0, TOPK_EXPERTS * r + jj]).start()
            return carry
        lax.fori_loop(0, tc, body, 0, unroll=4)

    @pl.when(i == 0)
    def _():
        start_rows(pos_ref, 0)

    @pl.when(i + 1 < pl.num_programs(0))
    def _():
        start_rows(nxt_ref, 1 - slot)

    for r in range(tc):
        for jj in range(TOPK_EXPERTS):
            row_copy(slot, r, jj, 0).wait()
    moe = ew_ref[:, 0:1] * buf[slot, 0] + ew_ref[:, 1:2] * buf[slot, 1]
    o_ref[...] = x_ref[...] + g_ref[0] * moe


def _moe_combine(pos, x2d, S, g2, ew_all, tok_off, ys):
    T, D = x2d.shape
    tc = _tile(T, (256, 128, 64))
    assert tok_off % tc == 0
    n = T // tc
    g, rows, bidx = _mod_operand(g2, S, tc)
    posb = pos.reshape(n, 1, TOPK_EXPERTS * tc)
    pspec = lambda nxt: pl.BlockSpec((1, 1, TOPK_EXPERTS * tc), lambda i: (jnp.minimum(i + nxt, n - 1), 0, 0),
                                     memory_space=pltpu.SMEM)
    return pl.pallas_call(
        functools.partial(_combine_kernel, tc=tc), name="moe_combine",
        out_shape=jax.ShapeDtypeStruct((T, D), F32),
        grid=(n,),
        in_specs=[pspec(0), pspec(1),
                  pl.BlockSpec((tc, D), lambda i: (i, 0)),
                  pl.BlockSpec((1, rows, D), lambda i: (bidx(i), 0, 0)),
                  pl.BlockSpec((tc, LANES), lambda i: (i + tok_off // tc, 0)),
                  pl.BlockSpec(memory_space=pl.ANY)],
        out_specs=pl.BlockSpec((tc, D), lambda i: (i, 0)),
        scratch_shapes=[pltpu.VMEM((2, TOPK_EXPERTS, tc, D), F32), pltpu.SemaphoreType.DMA((2,))],
        compiler_params=_cparams("arbitrary"),
    )(posb, posb, x2d, g, ew_all, ys)


def _route_meta(eid, NE, BM):
    n_slots = eid.shape[0] * TOPK_EXPERTS
    slot_e = eid.reshape(-1)
    slot_id = jnp.arange(n_slots, dtype=I32)
    se, order = lax.sort((slot_e, slot_id), num_keys=1, is_stable=True)
    experts = jnp.arange(NE, dtype=I32)
    onehot = se[:, None] == experts[None, :]
    counts = jnp.sum(onehot, axis=0, dtype=I32)
    start = jnp.cumsum(counts) - counts
    padded = (counts + BM - 1) // BM * BM
    pend = jnp.cumsum(padded)
    pstart = pend - padded
    P = -(-n_slots // BM) * BM + NE * BM
    nblk = P // BM
    blk_row = jnp.arange(nblk, dtype=I32) * BM
    block_e = jnp.minimum(jnp.sum(pend[None, :] <= blk_row[:, None], axis=1, dtype=I32), NE - 1)
    nused = (pend[-1] // BM).astype(I32).reshape(1)
    blk_id = jnp.arange(nblk, dtype=I32)
    used = blk_row < pend[-1]
    run_start = jnp.where(used, pstart[block_e] // BM, blk_id).astype(I32)
    run_len = jnp.where(used, padded[block_e] // BM, 1).astype(I32)
    dest_sorted = slot_id + jnp.sum(jnp.where(onehot, (pstart - start)[None, :], 0), axis=1, dtype=I32)
    _, pos = lax.sort((order, dest_sorted), num_keys=1)
    src0 = start[block_e] + blk_row - pstart[block_e]
    order_pad = jnp.concatenate([order, jnp.zeros((BM,), I32)])
    runs = jax.vmap(lambda s: lax.dynamic_slice(order_pad, (s,), (BM,)))(jnp.clip(src0, 0, n_slots))
    src = src0[:, None] + jnp.arange(BM, dtype=I32)[None, :]
    valid = src < (start + counts)[block_e][:, None]
    tok_buf = jnp.where(valid, runs // TOPK_EXPERTS, 0)
    tok_buf = jnp.where((blk_row < pend[-1])[:, None], tok_buf, -1).reshape(P).astype(I32)
    return tok_buf, pos, (block_e, run_start, run_len, nused)


def _layer_front(x, mod, conv_prev, past, W, dims):
    B, S, D = x.shape
    T = B * S
    NH, KVH, hd, NIH, DI, NG, NE = dims
    sh1, sc1, g1, sh2, sc2, g2 = jnp.split(mod, 6, axis=-1)
    x2d = x.reshape(T, D)
    xm = _modulate(x2d, S, W["norm1_g"], sh1, sc1)
    hbc = _matmul(xm, W["w_in_conv"], BF16, name="inproj_conv")
    raw = _matmul(xm, W["w_in_attn"], F32, name="inproj_attn")
    gates = _matmul(xm, W["w_in_gate"], BF16, act="sigmoid", name="inproj_gate")
    q, k, v, ik, kb, vb, ikb, iq, iw = _attn_post(raw, W["q_norm_g"], W["k_norm_g"], W["ik_norm_g"],
                                                  NH, KVH, hd, NIH, DI)
    C = W["conv_w"].shape[1]
    conv_y, conv_state = _short_conv(hbc.reshape(B, S, 3 * C), conv_prev, W["conv_w"])
    r3 = lambda a: a.reshape(B, S, a.shape[-1])
    if past is None:
        k_all, v_all, ik_all = r3(kb), r3(vb), r3(ikb)
        L_true, causal = S, True
    else:
        k_past, v_past, ik_past = past
        Pn = k_past.shape[1]
        L_true, causal = Pn + S, False
        cat = lambda old, new: jnp.concatenate([old.reshape(B, Pn, -1).astype(BF16), r3(new)], axis=1)
        k_all, v_all, ik_all = cat(k_past, kb), cat(v_past, vb), cat(ik_past, ikb)
    attn = _attention(r3(q), r3(iq), r3(iw), k_all, v_all, ik_all, NH=NH, KVH=KVH, hd=hd, NIH=NIH, DI=DI,
                      topk=min(TOPK_MAX, L_true // 4), causal=causal, L_true=L_true)
    merged = _merge(conv_y.reshape(T, C), attn.reshape(T, NH * hd), W["w_conv_out"], W["w_attn_out"], gates)
    x_mid = _outproj(merged, W["w_out"], x2d, S, g1)
    outs = (k.reshape(B, S, KVH, hd), v.reshape(B, S, KVH, hd), ik.reshape(B, S, DI), conv_state)
    return x_mid, (sh2, sc2, g2), outs


def kernel(x_prompt, x_sample, cache_k, cache_v, cache_idx_k, state_conv, c_prompt, c_sample, norm1_g, norm2_g, w_ada, b_ada, w_in, conv_w, q_norm_g, k_norm_g, ik_norm_g, w_conv_out, w_attn_out, w_out, w_router_group, b_router_group, w_router_expert, b_router_expert, w_gate, w_up, w_down):
    Bp, Sp, D = x_prompt.shape
    Bs, Ss, _ = x_sample.shape
    CW, C = conv_w.shape
    hd = q_norm_g.shape[0]
    DI = ik_norm_g.shape[0]
    KVH = cache_k.shape[2]
    NH = w_attn_out.shape[0] // hd
    NG = b_router_group.shape[0]
    NE = b_router_expert.shape[0]
    off_q = 3 * C
    off_iq = off_q + NH * hd + 2 * KVH * hd
    off_gate = w_in.shape[1] - 2 * D
    NIH = (off_gate - off_iq - DI) // (DI + 1)
    dims = (NH, KVH, hd, NIH, DI, NG, NE)

    na = off_gate - off_q
    na_pad = -(-na // LANES) * LANES
    rpad = LANES - NG - NE
    W = {
        "norm1_g": norm1_g, "norm2_g": norm2_g, "conv_w": conv_w,
        "q_norm_g": q_norm_g, "k_norm_g": k_norm_g, "ik_norm_g": ik_norm_g,
        "w_in_conv": w_in[:, :off_q].astype(BF16),
        "w_in_attn": jnp.pad(w_in[:, off_q:off_gate], ((0, 0), (0, na_pad - na))).astype(BF16),
        "w_in_gate": w_in[:, off_gate:].astype(BF16),
        "w_conv_out": w_conv_out.astype(BF16), "w_attn_out": w_attn_out.astype(BF16),
        "w_out": w_out.astype(BF16),
        "w_router": jnp.pad(jnp.concatenate([w_router_group, w_router_expert], axis=1), ((0, 0), (0, rpad))),
        "b_router": jnp.pad(jnp.concatenate([b_router_group, b_router_expert]), (0, rpad)).reshape(1, LANES),
    }

    nb = Bp + Bs
    c_all = jnp.pad(jnp.concatenate([c_prompt, c_sample], axis=0), ((0, -nb % 8), (0, 0)))
    mod = _ada(c_all, w_ada, b_ada)

    conv_zero = jnp.zeros((Bp, CW - 1, C), F32)
    xmid_p, (sh2_p, sc2_p, g2_p), outs_p = _layer_front(x_prompt, mod[:Bp], conv_zero, None, W, dims)
    xmid_s, (sh2_s, sc2_s, g2_s), outs_s = _layer_front(
        x_sample, mod[Bp:nb], state_conv, (cache_k, cache_v, cache_idx_k), W, dims)

    Tp = Bp * Sp
    xm2, eid, ew = _router([(xmid_p, Sp, sh2_p, sc2_p), (xmid_s, Ss, sh2_s, sc2_s)],
                           norm2_g, W["w_router"], W["b_router"], NG, NE)
    tok_buf, pos, blocks = _route_meta(eid[:, :TOPK_EXPERTS], NE, MOE_BM)
    xs = _moe_gather(tok_buf, xm2)
    ys = _moe_experts(*blocks, xs, w_gate, w_up, w_down, MOE_BM)
    y_p = _moe_combine(pos[:TOPK_EXPERTS * Tp], xmid_p, Sp, g2_p, ew, 0, ys).reshape(Bp, Sp, D)
    y_s = _moe_combine(pos[TOPK_EXPERTS * Tp:], xmid_s, Ss, g2_s, ew, Tp, ys).reshape(Bs, Ss, D)

    k_p, v_p, ik_p, conv_p = outs_p
    k_s, v_s, ik_s, conv_s = outs_s
    return (y_p, y_s, k_p, v_p, ik_p, conv_p, k_s, v_s, ik_s, conv_s)
```

```python
import functools

import jax
import jax.numpy as jnp
from jax import lax
from jax.experimental import pallas as pl
from jax.experimental.pallas import tpu as pltpu

F32 = jnp.float32
BF16 = jnp.bfloat16
I32 = jnp.int32

CHUNK = 64
EPS = 1e-6
TOPK_MAX = 256
TOPK_EXPERTS = 2
LANES = 128
VMEM_LIMIT = 56 * 1024 * 1024
INT_MIN = -2 ** 31
NEG = -1e30
LOG2E = 1.4426950408889634
MOE_BM = 512
ATTN_KEY_TILE = 256
IDX_HEADS_PER_DOT = 16
NT_DIMS = (((1,), (1,)), ((), ()))


def _tile(n, prefs):
    for t in prefs:
        if n % t == 0:
            return t
    return n


def _cparams(*sem):
    return pltpu.CompilerParams(dimension_semantics=sem, vmem_limit_bytes=VMEM_LIMIT)


def _mod_operand(m, S, tm):
    B, D = m.shape
    if S % tm == 0:
        return m.reshape(B, 1, D), 1, (lambda i: (i * tm) // S)
    assert tm % S == 0
    full = jnp.broadcast_to(m[:, None, :], (B, S, D)).reshape((B * S) // tm, tm, D)
    return full, tm, (lambda i: i)


def _ada_kernel(c_ref, w_ref, b_ref, o_ref):
    o_ref[...] = jnp.dot(c_ref[...], w_ref[...], preferred_element_type=F32,
                         precision=lax.Precision.HIGHEST) + b_ref[...]


def _ada(c, w, b):
    M, D = c.shape
    N = w.shape[1]
    tn = _tile(N, (512, 256, 128))
    return pl.pallas_call(
        _ada_kernel, name="ada",
        out_shape=jax.ShapeDtypeStruct((M, N), F32),
        grid=(N // tn,),
        in_specs=[pl.BlockSpec((M, D), lambda j: (0, 0)),
                  pl.BlockSpec((D, tn), lambda j: (0, j)),
                  pl.BlockSpec((1, tn), lambda j: (0, j))],
        out_specs=pl.BlockSpec((M, tn), lambda j: (0, j)),
        compiler_params=_cparams("arbitrary"),
    )(c, w, b.reshape(1, N))


def _modulated(x, g, scale, shift):
    ms = jnp.mean(x * x, axis=-1, keepdims=True)
    return (x * lax.rsqrt(ms + EPS)) * g * (1.0 + scale) + shift


def _modulate_kernel(x_ref, g_ref, sh_ref, sc_ref, o_ref):
    o_ref[...] = _modulated(x_ref[...], g_ref[...], sc_ref[0], sh_ref[0]).astype(o_ref.dtype)


def _modulate(x2d, S, g, shift, scale):
    T, D = x2d.shape
    tm = _tile(S, (512, 256, 128, 64))
    sh, rows, bidx = _mod_operand(shift, S, tm)
    sc, _, _ = _mod_operand(scale, S, tm)
    mspec = pl.BlockSpec((1, rows, D), lambda i: (bidx(i), 0, 0))
    return pl.pallas_call(
        _modulate_kernel, name="modulate",
        out_shape=jax.ShapeDtypeStruct((T, D), BF16),
        grid=(T // tm,),
        in_specs=[pl.BlockSpec((tm, D), lambda i: (i, 0)),
                  pl.BlockSpec((1, D), lambda i: (0, 0)), mspec, mspec],
        out_specs=pl.BlockSpec((tm, D), lambda i: (i, 0)),
        compiler_params=_cparams("arbitrary"),
    )(x2d, g.reshape(1, D), sh, sc)


def _mm_kernel(a_ref, b_ref, o_ref, *, act):
    acc = jnp.dot(a_ref[...], b_ref[...], preferred_element_type=F32)
    if act == "sigmoid":
        acc = jax.nn.sigmoid(acc)
    o_ref[...] = acc.astype(o_ref.dtype)


def _matmul(a, b, out_dtype, act="none", name="matmul"):
    M, K = a.shape
    N = b.shape[1]
    tm = _tile(M, (2048, 1024, 512, 256, 128))
    tn = _tile(N, (512, 384, 256, 128))
    return pl.pallas_call(
        functools.partial(_mm_kernel, act=act), name=name,
        out_shape=jax.ShapeDtypeStruct((M, N), out_dtype),
        grid=(M // tm, N // tn),
        in_specs=[pl.BlockSpec((tm, K), lambda i, j: (i, 0)),
                  pl.BlockSpec((K, tn), lambda i, j: (0, j))],
        out_specs=pl.BlockSpec((tm, tn), lambda i, j: (i, j)),
        compiler_params=_cparams("arbitrary", "arbitrary"),
    )(a, b)


def _attn_post_kernel(raw_ref, qg_ref, kg_ref, ikg_ref,
                      q_ref, k_ref, v_ref, ik_ref, kb_ref, vb_ref, ikb_ref, iq_ref, iw_ref,
                      *, NH, KVH, hd, NIH, DI):
    def headnorm(col, g):
        seg = raw_ref[:, col:col + hd]
        ms = jnp.mean(seg * seg, axis=-1, keepdims=True)
        return seg * lax.rsqrt(ms + EPS) * g

    qg = qg_ref[...] * (LOG2E * hd ** -0.5)
    for h in range(NH):
        q_ref[:, h * hd:(h + 1) * hd] = (headnorm(h * hd, 1.0) * qg).astype(q_ref.dtype)
    off_k = NH * hd
    off_v = off_k + KVH * hd
    for h in range(KVH):
        kn = headnorm(off_k + h * hd, kg_ref[...])
        k_ref[:, h * hd:(h + 1) * hd] = kn
        kb_ref[:, h * hd:(h + 1) * hd] = kn.astype(kb_ref.dtype)
    v = raw_ref[:, off_v:off_v + KVH * hd]
    v_ref[...] = v
    vb_ref[...] = v.astype(vb_ref.dtype)
    off_iq = off_v + KVH * hd
    off_ik = off_iq + NIH * DI
    off_iw = off_ik + DI
    iq_ref[...] = (raw_ref[:, off_iq:off_ik] * (DI ** -0.5)).astype(iq_ref.dtype)
    ikr = raw_ref[:, off_ik:off_iw]
    ikn = ikr * lax.rsqrt(jnp.mean(ikr * ikr, axis=-1, keepdims=True) + EPS) * ikg_ref[...]
    ik_ref[...] = ikn
    ikb_ref[...] = ikn.astype(ikb_ref.dtype)
    iw_ref[...] = raw_ref[:, off_iw:off_iw + NIH] * (NIH ** -0.5)


def _attn_post(raw, q_norm_g, k_norm_g, ik_norm_g, NH, KVH, hd, NIH, DI):
    T, NA = raw.shape
    tr = _tile(T, (256, 128, 64))
    outs = [(NH * hd, BF16), (KVH * hd, F32), (KVH * hd, F32), (DI, F32),
            (KVH * hd, BF16), (KVH * hd, BF16), (DI, BF16), (NIH * DI, BF16), (NIH, F32)]
    return pl.pallas_call(
        functools.partial(_attn_post_kernel, NH=NH, KVH=KVH, hd=hd, NIH=NIH, DI=DI), name="attn_post",
        out_shape=[jax.ShapeDtypeStruct((T, n), dt) for n, dt in outs],
        grid=(T // tr,),
        in_specs=[pl.BlockSpec((tr, NA), lambda i: (i, 0)),
                  pl.BlockSpec((1, hd), lambda i: (0, 0)),
                  pl.BlockSpec((1, hd), lambda i: (0, 0)),
                  pl.BlockSpec((1, DI), lambda i: (0, 0))],
        out_specs=[pl.BlockSpec((tr, n), lambda i: (i, 0)) for n, _ in outs],
        compiler_params=_cparams("arbitrary"),
    )(raw, q_norm_g.reshape(1, hd), k_norm_g.reshape(1, hd), ik_norm_g.reshape(1, DI))


def _conv_kernel(h_ref, b_ref, c_ref, w_ref, prev_ref, y_ref, st_ref, ubuf, *, ts, CW):
    lo = 8 - (CW - 1)

    @pl.when(pl.program_id(1) == 0)
    def _():
        ubuf[lo:8, :] = prev_ref[0]

    @pl.when(pl.program_id(1) > 0)
    def _():
        ubuf[lo:8, :] = ubuf[ts + lo:ts + 8, :]

    ubuf[8:8 + ts, :] = c_ref[0].astype(F32) * h_ref[0].astype(F32)
    y = w_ref[0:1, :] * ubuf[lo:lo + ts, :]
    for j in range(1, CW):
        y = y + w_ref[j:j + 1, :] * ubuf[lo + j:lo + j + ts, :]
    y_ref[0] = (b_ref[0].astype(F32) * y).astype(y_ref.dtype)
    st_ref[0] = ubuf[ts + lo:ts + 8, :]


def _short_conv(hbc, conv_prev, conv_w):
    B, S, C3 = hbc.shape
    CW, C = conv_w.shape
    ts = _tile(S, (512, 256, 128, 64))
    col = lambda k: pl.BlockSpec((1, ts, C), lambda b, s: (b, s, k))
    return pl.pallas_call(
        functools.partial(_conv_kernel, ts=ts, CW=CW), name="short_conv",
        out_shape=[jax.ShapeDtypeStruct((B, S, C), BF16), jax.ShapeDtypeStruct((B, CW - 1, C), F32)],
        grid=(B, S // ts),
        in_specs=[col(0), col(1), col(2),
                  pl.BlockSpec((CW, C), lambda b, s: (0, 0)),
                  pl.BlockSpec((1, CW - 1, C), lambda b, s: (b, 0, 0))],
        out_specs=[pl.BlockSpec((1, ts, C), lambda b, s: (b, s, 0)),
                   pl.BlockSpec((1, CW - 1, C), lambda b, s: (b, 0, 0))],
        scratch_shapes=[pltpu.VMEM((ts + 8, C), F32)],
        compiler_params=_cparams("arbitrary", "arbitrary"),
    )(hbc, hbc, hbc, conv_w, conv_prev)


def _attn_kernel(q_ref, iq_ref, iwt_ref, k_ref, vt_ref, ik_ref, o_ref, keys_ref, bias_sc, m_sc, l_sc, acc_sc,
                 *, QB, TL, NH, KVH, hd, NIH, DI, topk, causal, L_true, Lp):
    i = pl.program_id(1)
    rep = NH // KVH
    if causal:
        qrow = i * QB + lax.broadcasted_iota(I32, (1, QB), 1)
        vis = (qrow // CHUNK + 1) * CHUNK
        nkt = ((i + 1) * QB + TL - 1) // TL
    else:
        vis = jnp.full((1, QB), L_true, I32)
        nkt = Lp // TL

    def score_tile(t, carry):
        off = pl.multiple_of(t * TL, TL)
        ikt = ik_ref[0, pl.ds(off, TL), :]
        acc = jnp.zeros((TL, QB), F32)
        for h0 in range(0, NIH, IDX_HEADS_PER_DOT):
            nh = min(IDX_HEADS_PER_DOT, NIH - h0)
            d = lax.dot_general(ikt, iq_ref[0, 0, h0 * QB:(h0 + nh) * QB, :], NT_DIMS, preferred_element_type=F32)
            for h in range(h0, h0 + nh):
                acc = acc + iwt_ref[0, h:h + 1, :] * jnp.maximum(d[:, (h - h0) * QB:(h - h0 + 1) * QB], 0.0)
        kid = off + lax.broadcasted_iota(I32, (TL, 1), 0)
        keys_ref[pl.ds(off, TL), :] = jnp.where(kid < vis, acc, -jnp.inf)
        return carry

    lax.fori_loop(0, nkt, score_tile, 0)

    def ordered_bits_to_f32(k):
        return pltpu.bitcast(k ^ ((k >> 31) & jnp.int32(0x7FFFFFFF)), F32)

    def count_where(pred):
        def body(t, c):
            off = pl.multiple_of(t * TL, TL)
            hit = jnp.where(pred(keys_ref[pl.ds(off, TL), :], off), 1, 0).astype(I32)
            return c + jnp.sum(hit.reshape(TL // 8, 8, QB), axis=0)
        c = lax.fori_loop(0, nkt, body, jnp.zeros((8, QB), I32))
        return jnp.sum(c, axis=0, keepdims=True)

    def bit_step(bi, thr_bits):
        cand_bits = thr_bits + (jnp.int32(1) << (31 - bi))
        cand = ordered_bits_to_f32(cand_bits)
        return jnp.where(count_where(lambda s, off: s >= cand) >= topk, cand_bits, thr_bits)

    thr = ordered_bits_to_f32(lax.fori_loop(0, 32, bit_step, jnp.full((1, QB), INT_MIN, I32)))
    thr = jnp.where(thr > -jnp.inf, thr, jnp.finfo(F32).min)

    def tie_search(_):
        need = topk - count_where(lambda s, off: s > thr)

        def count_eq_below(cand):
            return count_where(
                lambda s, off: (s == thr) & (off + lax.broadcasted_iota(I32, (TL, 1), 0) < cand))

        def idx_step(bi, lo):
            cand = lo + (jnp.int32(1) << (IDX_BITS - 1 - bi))
            return jnp.where(count_eq_below(cand) < need, cand, lo)

        return lax.fori_loop(0, IDX_BITS, idx_step, jnp.zeros((1, QB), I32))

    IDX_BITS = max(1, Lp.bit_length())
    tie_idx = lax.cond(jnp.max(count_where(lambda s, off: s >= thr)) > topk, tie_search,
                       lambda _: jnp.full((1, QB), Lp, I32), 0)

    m_sc[...] = jnp.full(m_sc.shape, NEG, F32)
    l_sc[...] = jnp.zeros(l_sc.shape, F32)
    acc_sc[...] = jnp.zeros(acc_sc.shape, F32)

    def kv_tile(t, carry):
        off = pl.multiple_of(t * TL, TL)
        kid = off + lax.broadcasted_iota(I32, (TL, 1), 0)
        kt_keys = keys_ref[pl.ds(off, TL), :]
        keep = (kt_keys > thr) | ((kt_keys == thr) & (kid <= tie_idx))
        bias_sc[...] = jnp.where(keep, 0.0, NEG)
        for g in range(KVH):
            kt = k_ref[0, pl.ds(off, TL), g * hd:(g + 1) * hd]
            vt = vt_ref[0, g * hd:(g + 1) * hd, pl.ds(off, TL)]
            for r in range(rep):
                h = g * rep + r
                s = lax.dot_general(kt, q_ref[0, :, h * hd:(h + 1) * hd], NT_DIMS,
                                    preferred_element_type=F32) + bias_sc[...]
                m_old = m_sc[h:h + 1, :]
                m_new = jnp.maximum(m_old, jnp.max(s, axis=0, keepdims=True))
                alpha = jnp.exp2(m_old - m_new)
                p = jnp.exp2(s - m_new)
                l_sc[h:h + 1, :] = alpha * l_sc[h:h + 1, :] + jnp.sum(p, axis=0, keepdims=True)
                acc_sc[h] = alpha * acc_sc[h] + jnp.dot(vt, p.astype(vt.dtype), preferred_element_type=F32)
                m_sc[h:h + 1, :] = m_new
        return carry

    lax.fori_loop(0, nkt, kv_tile, 0)
    for h in range(NH):
        o = acc_sc[h] / l_sc[h:h + 1, :]
        o_ref[0, :, h * hd:(h + 1) * hd] = o.T.astype(o_ref.dtype)


def _attention(q, iq, iw, k_all, v_all, ik_all, *, NH, KVH, hd, NIH, DI, topk, causal, L_true):
    B, S0, _ = q.shape
    QB = LANES
    S = -(-S0 // QB) * QB
    TL = ATTN_KEY_TILE
    Lp = -(-k_all.shape[1] // TL) * TL
    pad_s = lambda a: jnp.pad(a, ((0, 0), (0, S - S0), (0, 0)))
    pad_l = lambda a: jnp.pad(a, ((0, 0), (0, Lp - a.shape[1]), (0, 0)))
    q, iq, iw = pad_s(q), pad_s(iq), pad_s(iw)
    k_all, v_all, ik_all = pad_l(k_all), pad_l(v_all), pad_l(ik_all)
    iq_heads = iq.reshape(B, S // QB, QB, NIH, DI).transpose(0, 1, 3, 2, 4).reshape(B, S // QB, NIH * QB, DI)
    qspec = lambda n: pl.BlockSpec((1, QB, n), lambda b, i: (b, i, 0))
    kspec = lambda n: pl.BlockSpec((1, Lp, n), lambda b, i: (b, 0, 0))
    out = pl.pallas_call(
        functools.partial(_attn_kernel, QB=QB, TL=TL, NH=NH, KVH=KVH, hd=hd, NIH=NIH, DI=DI,
                          topk=topk, causal=causal, L_true=L_true, Lp=Lp), name="sparse_attn",
        out_shape=jax.ShapeDtypeStruct((B, S, NH * hd), BF16),
        grid=(B, S // QB),
        in_specs=[qspec(NH * hd),
                  pl.BlockSpec((1, 1, NIH * QB, DI), lambda b, i: (b, i, 0, 0)),
                  pl.BlockSpec((1, NIH, QB), lambda b, i: (b, 0, i)),
                  kspec(KVH * hd),
                  pl.BlockSpec((1, KVH * hd, Lp), lambda b, i: (b, 0, 0)),
                  kspec(DI)],
        out_specs=qspec(NH * hd),
        scratch_shapes=[pltpu.VMEM((Lp, QB), F32), pltpu.VMEM((TL, QB), F32), pltpu.VMEM((NH, QB), F32),
                        pltpu.VMEM((NH, QB), F32), pltpu.VMEM((NH, hd, QB), F32)],
        compiler_params=_cparams("arbitrary", "arbitrary"),
    )(q, iq_heads, jnp.swapaxes(iw, 1, 2), k_all, jnp.swapaxes(v_all, 1, 2), ik_all)
    return out[:, :S0]


def _merge_kernel(cy_ref, at_ref, wc_ref, wa_ref, gc_ref, ga_ref, o_ref):
    a = jnp.dot(cy_ref[...], wc_ref[...], preferred_element_type=F32)
    b = jnp.dot(at_ref[...], wa_ref[...], preferred_element_type=F32)
    o_ref[...] = (gc_ref[...].astype(F32) * a + ga_ref[...].astype(F32) * b).astype(o_ref.dtype)


def _merge(conv_y, attn, w_conv_out, w_attn_out, gates):
    T, C = conv_y.shape
    A = attn.shape[1]
    D = w_conv_out.shape[1]
    tm = _tile(T, (1024, 512, 256, 128))
    tn = _tile(D, (512, 256, 128))
    nj = D // tn
    return pl.pallas_call(
        _merge_kernel, name="merge",
        out_shape=jax.ShapeDtypeStruct((T, D), BF16),
        grid=(T // tm, nj),
        in_specs=[pl.BlockSpec((tm, C), lambda i, j: (i, 0)),
                  pl.BlockSpec((tm, A), lambda i, j: (i, 0)),
                  pl.BlockSpec((C, tn), lambda i, j: (0, j)),
                  pl.BlockSpec((A, tn), lambda i, j: (0, j)),
                  pl.BlockSpec((tm, tn), lambda i, j: (i, j)),
                  pl.BlockSpec((tm, tn), lambda i, j: (i, j + nj))],
        out_specs=pl.BlockSpec((tm, tn), lambda i, j: (i, j)),
        compiler_params=_cparams("arbitrary", "arbitrary"),
    )(conv_y, attn, w_conv_out, w_attn_out, gates, gates)


def _outproj_kernel(a_ref, w_ref, x_ref, g_ref, o_ref):
    acc = jnp.dot(a_ref[...], w_ref[...], preferred_element_type=F32)
    o_ref[...] = x_ref[...] + g_ref[0] * acc


def _outproj(merged, w_out, x2d, S, g1):
    T, D = x2d.shape
    K = merged.shape[1]
    tm = _tile(T, (1024, 512, 256, 128))
    tn = _tile(D, (512, 256, 128))
    g, rows, bidx = _mod_operand(g1, S, tm)
    return pl.pallas_call(
        _outproj_kernel, name="outproj",
        out_shape=jax.ShapeDtypeStruct((T, D), F32),
        grid=(T // tm, D // tn),
        in_specs=[pl.BlockSpec((tm, K), lambda i, j: (i, 0)),
                  pl.BlockSpec((K, tn), lambda i, j: (0, j)),
                  pl.BlockSpec((tm, tn), lambda i, j: (i, j)),
                  pl.BlockSpec((1, rows, tn), lambda i, j: (bidx(i), 0, j))],
        out_specs=pl.BlockSpec((tm, tn), lambda i, j: (i, j)),
        compiler_params=_cparams("arbitrary", "arbitrary"),
    )(merged, w_out, x2d, g)


def _pack_bf16_pairs(x):
    half = x.shape[-1] // 2
    lo = pltpu.bitcast(x[:, :half].astype(BF16).astype(F32), I32)
    hi = pltpu.bitcast(x[:, half:].astype(BF16).astype(F32), I32)
    return ((lo >> 16) & jnp.int32(0xFFFF)) | (hi & jnp.int32(-65536))


def _unpack_bf16_pairs(w):
    lo = pltpu.bitcast(w << 16, F32).astype(BF16)
    hi = pltpu.bitcast(w & jnp.int32(-65536), F32).astype(BF16)
    return jnp.concatenate([lo, hi], axis=-1)


def _router_kernel(*refs, NG, NE, nblk):
    npath = len(nblk)
    shared = refs[3 * npath:]
    i = pl.program_id(0)
    lo = 0
    for p in range(npath):
        @pl.when((i >= lo) & (i < lo + nblk[p]))
        def _(path_refs=refs[3 * p:3 * p + 3]):
            _route_rows(*path_refs, *shared, NG=NG, NE=NE)
        lo += nblk[p]


def _route_rows(x_ref, sh_ref, sc_ref, g_ref, wr_ref, br_ref, xm_ref, eid_ref, ew_ref, *, NG, NE):
    xm = _modulated(x_ref[...], g_ref[...], sc_ref[0], sh_ref[0])
    xm_ref[...] = _pack_bf16_pairs(xm)
    logits = jnp.dot(xm, wr_ref[...], preferred_element_type=F32,
                     precision=lax.Precision.HIGHEST) + br_ref[...]
    tm = logits.shape[0]
    epg = NE // NG
    lane = lax.broadcasted_iota(I32, (tm, LANES), 1)
    ninf = -jnp.inf

    def first_argmax(vals):
        mx = jnp.max(vals, axis=1, keepdims=True)
        return mx, jnp.min(jnp.where(vals == mx, lane, LANES), axis=1, keepdims=True)

    gmask = lane < NG
    gmax, g_sel = first_argmax(jnp.where(gmask, logits, ninf))
    g_w = 1.0 / jnp.sum(jnp.where(gmask, jnp.exp(logits - gmax), 0.0), axis=1, keepdims=True)
    emask = (lane >= NG) & (lane < NG + NE) & ((lane - NG) // epg == g_sel)
    el = jnp.where(emask, logits, ninf)
    m1, i1 = first_argmax(el)
    m2, i2 = first_argmax(jnp.where(lane == i1, ninf, el))
    r = jnp.exp(m2 - m1)
    w1 = g_w / (1.0 + r)
    w2 = w1 * r
    eid_ref[...] = jnp.where(lane == 0, i1 - NG, jnp.where(lane == 1, i2 - NG, 0))
    ew_ref[...] = jnp.where(lane == 0, w1, jnp.where(lane == 1, w2, 0.0))


def _router(paths, g, wr, br, NG, NE):
    D = paths[0][0].shape[1]
    tm = next(t for t in (256, 128, 64, 32, 16, 8)
              if all(x.shape[0] % t == 0 and (S % t == 0 or t % S == 0) for x, S, _, _ in paths))
    nblk = tuple(x.shape[0] // tm for x, _, _, _ in paths)
    T = sum(x.shape[0] for x, _, _, _ in paths)
    operands, in_specs, lo = [], [], 0
    for (x, S, shift, scale), nb in zip(paths, nblk):
        sh, rows, bidx = _mod_operand(shift, S, tm)
        sc, _, _ = _mod_operand(scale, S, tm)
        local = lambda i, lo=lo, nb=nb: jnp.clip(i - lo, 0, nb - 1)
        mspec = pl.BlockSpec((1, rows, D), lambda i, local=local, bidx=bidx: (bidx(local(i)), 0, 0))
        operands += [x, sh, sc]
        in_specs += [pl.BlockSpec((tm, D), lambda i, local=local: (local(i), 0)), mspec, mspec]
        lo += nb
    rowspec = lambda n: pl.BlockSpec((tm, n), lambda i: (i, 0))
    return pl.pallas_call(
        functools.partial(_router_kernel, NG=NG, NE=NE, nblk=nblk), name="router",
        out_shape=[jax.ShapeDtypeStruct((T, D // 2), I32), jax.ShapeDtypeStruct((T, LANES), I32),
                   jax.ShapeDtypeStruct((T, LANES), F32)],
        grid=(sum(nblk),),
        in_specs=in_specs + [pl.BlockSpec((1, D), lambda i: (0, 0)),
                             pl.BlockSpec((D, LANES), lambda i: (0, 0)), pl.BlockSpec((1, LANES), lambda i: (0, 0))],
        out_specs=[rowspec(D // 2), rowspec(LANES), rowspec(LANES)],
        compiler_params=_cparams("arbitrary"),
    )(*operands, g.reshape(1, D), wr, br)


def _gather_kernel(tok_ref, nxt_ref, x_hbm, o_ref, buf, sem, *, R):
    i = pl.program_id(0)
    slot = i % 2

    def row_copy(s, r, t):
        return pltpu.make_async_copy(x_hbm.at[pl.ds(t, 1)], buf.at[s, pl.ds(r, 1)], sem.at[s])

    def start_rows(tref, s):
        def body(r2, carry):
            for k in range(2):
                r = 2 * r2 + k
                row_copy(s, r, tref[0, 0, r]).start(priority=k)
            return carry
        lax.fori_loop(0, R // 2, body, 0, unroll=4)

    @pl.when((i == 0) & (tok_ref[0, 0, 0] >= 0))
    def _():
        start_rows(tok_ref, 0)

    @pl.when((i + 1 < pl.num_programs(0)) & (nxt_ref[0, 0, 0] >= 0))
    def _():
        start_rows(nxt_ref, 1 - slot)

    @pl.when(tok_ref[0, 0, 0] >= 0)
    def _():
        for r in range(R):
            row_copy(slot, r, 0).wait()
        o_ref[...] = buf[slot]

    @pl.when(tok_ref[0, 0, 0] < 0)
    def _():
        o_ref[...] = jnp.zeros(o_ref.shape, o_ref.dtype)


def _moe_gather(tok_buf, x):
    P = tok_buf.shape[0]
    D = x.shape[1]
    R = _tile(P, (512, 256, 128, 64))
    n = P // R
    toks = tok_buf.reshape(n, 1, R)
    return pl.pallas_call(
        functools.partial(_gather_kernel, R=R), name="moe_gather",
        out_shape=jax.ShapeDtypeStruct((P, D), x.dtype),
        grid=(n,),
        in_specs=[pl.BlockSpec((1, 1, R), lambda i: (i, 0, 0), memory_space=pltpu.SMEM),
                  pl.BlockSpec((1, 1, R), lambda i: (jnp.minimum(i + 1, n - 1), 0, 0), memory_space=pltpu.SMEM),
                  pl.BlockSpec(memory_space=pl.ANY)],
        out_specs=pl.BlockSpec((R, D), lambda i: (i, 0)),
        scratch_shapes=[pltpu.VMEM((2, R, D), x.dtype), pltpu.SemaphoreType.DMA((2,))],
        compiler_params=_cparams("arbitrary"),
    )(toks, toks, x)


def _expert_kernel(be_ref, nu_ref, xs_ref, wg_ref, wu_ref, wd_ref, o_ref, x_sc):
    i = pl.program_id(0)
    j = pl.program_id(1)

    @pl.when(j == 0)
    def _():
        o_ref[...] = jnp.zeros(o_ref.shape, o_ref.dtype)

    @pl.when((j == 0) & (i < nu_ref[0]))
    def _():
        x_sc[...] = _unpack_bf16_pairs(xs_ref[...])

    @pl.when(i < nu_ref[0])
    def _():
        x = x_sc[...]
        gate = jnp.dot(x, wg_ref[0].astype(BF16), preferred_element_type=F32)
        up = jnp.dot(x, wu_ref[0].astype(BF16), preferred_element_type=F32)
        hidden = (gate * jax.nn.sigmoid(gate)) * up
        o_ref[...] += jnp.dot(hidden.astype(BF16), wd_ref[0].astype(BF16), preferred_element_type=F32)


def _moe_experts(block_e, nused, xs, w_gate, w_up, w_down, BM):
    P = xs.shape[0]
    NE, D, DE = w_gate.shape
    tf = _tile(DE, (256, 128))
    nf = DE // tf
    row = lambda i, nu: jnp.minimum(i, nu[0] - 1)
    fj = lambda i, j, nu: jnp.where(i < nu[0], j, nf - 1)
    grid_spec = pltpu.PrefetchScalarGridSpec(
        num_scalar_prefetch=2,
        grid=(P // BM, nf),
        in_specs=[pl.BlockSpec((BM, D // 2), lambda i, j, be, nu: (row(i, nu), 0)),
                  pl.BlockSpec((1, D, tf), lambda i, j, be, nu: (be[row(i, nu)], 0, fj(i, j, nu))),
                  pl.BlockSpec((1, D, tf), lambda i, j, be, nu: (be[row(i, nu)], 0, fj(i, j, nu))),
                  pl.BlockSpec((1, tf, D), lambda i, j, be, nu: (be[row(i, nu)], fj(i, j, nu), 0))],
        out_specs=pl.BlockSpec((BM, D), lambda i, j, be, nu: (i, 0)),
        scratch_shapes=[pltpu.VMEM((BM, D), BF16)],
    )
    return pl.pallas_call(
        _expert_kernel, name="moe_experts", grid_spec=grid_spec,
        out_shape=jax.ShapeDtypeStruct((P, D), F32),
        compiler_params=_cparams("arbitrary", "arbitrary"),
    )(block_e, nused, xs, w_gate, w_up, w_down)


def _combine_kernel(pos_ref, nxt_ref, x_ref, g_ref, ew_ref, ys_hbm, o_ref, buf, sem, *, tc):
    i = pl.program_id(0)
    slot = i % 2

    def row_copy(s, r, jj, p):
        return pltpu.make_async_copy(ys_hbm.at[pl.ds(p, 1)], buf.at[s, jj, pl.ds(r, 1)], sem.at[s])

    def start_rows(pref, s):
        def body(r, carry):
            for jj in range(TOPK_EXPERTS):
                row_copy(s, r, jj, pref[0, 0, TOPK_EXPERTS * r + jj]).start(priority=jj % 2)
            return carry
        lax.fori_loop(0, tc, body, 0, unroll=4)

    @pl.when(i == 0)
    def _():
        start_rows(pos_ref, 0)

    @pl.when(i + 1 < pl.num_programs(0))
    def _():
        start_rows(nxt_ref, 1 - slot)

    for r in range(tc):
        for jj in range(TOPK_EXPERTS):
            row_copy(slot, r, jj, 0).wait()
    moe = ew_ref[:, 0:1] * buf[slot, 0] + ew_ref[:, 1:2] * buf[slot, 1]
    o_ref[...] = x_ref[...] + g_ref[0] * moe


def _moe_combine(pos, x2d, S, g2, ew_all, tok_off, ys):
    T, D = x2d.shape
    tc = _tile(T, (256, 128, 64))
    assert tok_off % tc == 0
    n = T // tc
    g, rows, bidx = _mod_operand(g2, S, tc)
    posb = pos.reshape(n, 1, TOPK_EXPERTS * tc)
    pspec = lambda nxt: pl.BlockSpec((1, 1, TOPK_EXPERTS * tc), lambda i: (jnp.minimum(i + nxt, n - 1), 0, 0),
                                     memory_space=pltpu.SMEM)
    return pl.pallas_call(
        functools.partial(_combine_kernel, tc=tc), name="moe_combine",
        out_shape=jax.ShapeDtypeStruct((T, D), F32),
        grid=(n,),
        in_specs=[pspec(0), pspec(1),
                  pl.BlockSpec((tc, D), lambda i: (i, 0)),
                  pl.BlockSpec((1, rows, D), lambda i: (bidx(i), 0, 0)),
                  pl.BlockSpec((tc, LANES), lambda i: (i + tok_off // tc, 0)),
                  pl.BlockSpec(memory_space=pl.ANY)],
        out_specs=pl.BlockSpec((tc, D), lambda i: (i, 0)),
        scratch_shapes=[pltpu.VMEM((2, TOPK_EXPERTS, tc, D), F32), pltpu.SemaphoreType.DMA((2,))],
        compiler_params=_cparams("arbitrary"),
    )(posb, posb, x2d, g, ew_all, ys)


def _route_meta(eid, NE, BM):
    n_slots = eid.shape[0] * TOPK_EXPERTS
    slot_e = eid.reshape(-1)
    slot_id = jnp.arange(n_slots, dtype=I32)
    se, order = lax.sort((slot_e, slot_id), num_keys=1, is_stable=True)
    experts = jnp.arange(NE, dtype=I32)
    onehot = se[:, None] == experts[None, :]
    counts = jnp.sum(onehot, axis=0, dtype=I32)
    start = jnp.cumsum(counts) - counts
    padded = (counts + BM - 1) // BM * BM
    pend = jnp.cumsum(padded)
    pstart = pend - padded
    P = -(-n_slots // BM) * BM + NE * BM
    nblk = P // BM
    blk_row = jnp.arange(nblk, dtype=I32) * BM
    block_e = jnp.minimum(jnp.sum(pend[None, :] <= blk_row[:, None], axis=1, dtype=I32), NE - 1)
    nused = (pend[-1] // BM).astype(I32).reshape(1)
    dest_sorted = slot_id + jnp.sum(jnp.where(onehot, (pstart - start)[None, :], 0), axis=1, dtype=I32)
    _, pos = lax.sort((order, dest_sorted), num_keys=1)
    src0 = start[block_e] + blk_row - pstart[block_e]
    order_pad = jnp.concatenate([order, jnp.zeros((BM,), I32)])
    runs = jax.vmap(lambda s: lax.dynamic_slice(order_pad, (s,), (BM,)))(jnp.clip(src0, 0, n_slots))
    src = src0[:, None] + jnp.arange(BM, dtype=I32)[None, :]
    valid = src < (start + counts)[block_e][:, None]
    tok_buf = jnp.where(valid, runs // TOPK_EXPERTS, 0)
    tok_buf = jnp.where((blk_row < pend[-1])[:, None], tok_buf, -1).reshape(P).astype(I32)
    return tok_buf, pos, block_e, nused


def _layer_front(x, mod, conv_prev, past, W, dims):
    B, S, D = x.shape
    T = B * S
    NH, KVH, hd, NIH, DI, NG, NE = dims
    sh1, sc1, g1, sh2, sc2, g2 = jnp.split(mod, 6, axis=-1)
    x2d = x.reshape(T, D)
    xm = _modulate(x2d, S, W["norm1_g"], sh1, sc1)
    hbc = _matmul(xm, W["w_in_conv"], BF16, name="inproj_conv")
    raw = _matmul(xm, W["w_in_attn"], F32, name="inproj_attn")
    gates = _matmul(xm, W["w_in_gate"], BF16, act="sigmoid", name="inproj_gate")
    q, k, v, ik, kb, vb, ikb, iq, iw = _attn_post(raw, W["q_norm_g"], W["k_norm_g"], W["ik_norm_g"],
                                                  NH, KVH, hd, NIH, DI)
    C = W["conv_w"].shape[1]
    conv_y, conv_state = _short_conv(hbc.reshape(B, S, 3 * C), conv_prev, W["conv_w"])
    r3 = lambda a: a.reshape(B, S, a.shape[-1])
    if past is None:
        k_all, v_all, ik_all = r3(kb), r3(vb), r3(ikb)
        L_true, causal = S, True
    else:
        k_past, v_past, ik_past = past
        Pn = k_past.shape[1]
        L_true, causal = Pn + S, False
        cat = lambda old, new: jnp.concatenate([old.reshape(B, Pn, -1).astype(BF16), r3(new)], axis=1)
        k_all, v_all, ik_all = cat(k_past, kb), cat(v_past, vb), cat(ik_past, ikb)
    attn = _attention(r3(q), r3(iq), r3(iw), k_all, v_all, ik_all, NH=NH, KVH=KVH, hd=hd, NIH=NIH, DI=DI,
                      topk=min(TOPK_MAX, L_true // 4), causal=causal, L_true=L_true)
    merged = _merge(conv_y.reshape(T, C), attn.reshape(T, NH * hd), W["w_conv_out"], W["w_attn_out"], gates)
    x_mid = _outproj(merged, W["w_out"], x2d, S, g1)
    outs = (k.reshape(B, S, KVH, hd), v.reshape(B, S, KVH, hd), ik.reshape(B, S, DI), conv_state)
    return x_mid, (sh2, sc2, g2), outs


def kernel(x_prompt, x_sample, cache_k, cache_v, cache_idx_k, state_conv, c_prompt, c_sample, norm1_g, norm2_g, w_ada, b_ada, w_in, conv_w, q_norm_g, k_norm_g, ik_norm_g, w_conv_out, w_attn_out, w_out, w_router_group, b_router_group, w_router_expert, b_router_expert, w_gate, w_up, w_down):
    Bp, Sp, D = x_prompt.shape
    Bs, Ss, _ = x_sample.shape
    CW, C = conv_w.shape
    hd = q_norm_g.shape[0]
    DI = ik_norm_g.shape[0]
    KVH = cache_k.shape[2]
    NH = w_attn_out.shape[0] // hd
    NG = b_router_group.shape[0]
    NE = b_router_expert.shape[0]
    off_q = 3 * C
    off_iq = off_q + NH * hd + 2 * KVH * hd
    off_gate = w_in.shape[1] - 2 * D
    NIH = (off_gate - off_iq - DI) // (DI + 1)
    dims = (NH, KVH, hd, NIH, DI, NG, NE)

    na = off_gate - off_q
    na_pad = -(-na // LANES) * LANES
    rpad = LANES - NG - NE
    W = {
        "norm1_g": norm1_g, "norm2_g": norm2_g, "conv_w": conv_w,
        "q_norm_g": q_norm_g, "k_norm_g": k_norm_g, "ik_norm_g": ik_norm_g,
        "w_in_conv": w_in[:, :off_q].astype(BF16),
        "w_in_attn": jnp.pad(w_in[:, off_q:off_gate], ((0, 0), (0, na_pad - na))).astype(BF16),
        "w_in_gate": w_in[:, off_gate:].astype(BF16),
        "w_conv_out": w_conv_out.astype(BF16), "w_attn_out": w_attn_out.astype(BF16),
        "w_out": w_out.astype(BF16),
        "w_router": jnp.pad(jnp.concatenate([w_router_group, w_router_expert], axis=1), ((0, 0), (0, rpad))),
        "b_router": jnp.pad(jnp.concatenate([b_router_group, b_router_expert]), (0, rpad)).reshape(1, LANES),
    }

    nb = Bp + Bs
    c_all = jnp.pad(jnp.concatenate([c_prompt, c_sample], axis=0), ((0, -nb % 8), (0, 0)))
    mod = _ada(c_all, w_ada, b_ada)

    conv_zero = jnp.zeros((Bp, CW - 1, C), F32)
    xmid_p, (sh2_p, sc2_p, g2_p), outs_p = _layer_front(x_prompt, mod[:Bp], conv_zero, None, W, dims)
    xmid_s, (sh2_s, sc2_s, g2_s), outs_s = _layer_front(
        x_sample, mod[Bp:nb], state_conv, (cache_k, cache_v, cache_idx_k), W, dims)

    Tp = Bp * Sp
    xm2, eid, ew = _router([(xmid_p, Sp, sh2_p, sc2_p), (xmid_s, Ss, sh2_s, sc2_s)],
                           norm2_g, W["w_router"], W["b_router"], NG, NE)
    tok_buf, pos, block_e, nused = _route_meta(eid[:, :TOPK_EXPERTS], NE, MOE_BM)
    xs = _moe_gather(tok_buf, xm2)
    ys = _moe_experts(block_e, nused, xs, w_gate, w_up, w_down, MOE_BM)
    y_p = _moe_combine(pos[:TOPK_EXPERTS * Tp], xmid_p, Sp, g2_p, ew, 0, ys).reshape(Bp, Sp, D)
    y_s = _moe_combine(pos[TOPK_EXPERTS * Tp:], xmid_s, Ss, g2_s, ew, Tp, ys).reshape(Bs, Ss, D)

    k_p, v_p, ik_p, conv_p = outs_p
    k_s, v_s, ik_s, conv_s = outs_s
    return (y_p, y_s, k_p, v_p, ik_p, conv_p, k_s, v_s, ik_s, conv_s)
```
